```python
import math
import jax, jax.numpy as jnp
from jax import lax
import numpy as np

D_MODEL = 2048
BATCH = 8
SEQ = 4096
DEPTH = 1

CHUNK = 64
A_HEADS = 8
A_HEAD_DIM = 128
A_LEFT_CHUNKS = 8
REL_CLIP_LEFT = 256
REL_CLIP_RIGHT = CHUNK - 1
REL_SIZE = REL_CLIP_LEFT + REL_CLIP_RIGHT + 1
B_HEADS = 8
Q_LORA = 512
KV_LORA = 512
QK_NOPE = 128
QK_ROPE = 64
V_HEAD = 128
ROPE_THETA = 10000.0
Q_BLOCK = 128
N_KEYS = 128
N_EXPERTS = N_KEYS * N_KEYS
PEER_HEADS = 8
PEER_TOPK = 16
PEER_QDIM = 256
PEER_HALF = PEER_QDIM // 2
TOKEN_BLOCK = 128
DN_ALPHA = (2 * DEPTH) ** 0.25
DN_BETA = (8 * DEPTH) ** -0.25
LN_EPS = 1e-5
RMS_EPS = 1e-6
NEG = -1e30

A_WIDTH = A_HEADS * A_HEAD_DIM
B_WIDTH = B_HEADS * V_HEAD
IN_SIZES = [A_WIDTH, A_WIDTH, A_WIDTH, Q_LORA, KV_LORA, QK_ROPE, D_MODEL, D_MODEL]
IN_COLS = sum(IN_SIZES)

kernel_name = "hybrid_chunkattn_mla_peer_deepnorm_adaln"


def layer_norm(x, g, b):
    xf = x.astype(jnp.float32)
    mu = jnp.mean(xf, axis=-1, keepdims=True)
    var = jnp.mean(jnp.square(xf - mu), axis=-1, keepdims=True)
    y = (xf - mu) * lax.rsqrt(var + LN_EPS)
    return (y * g.astype(jnp.float32) + b.astype(jnp.float32)).astype(x.dtype)


def rms_norm(x, g):
    xf = x.astype(jnp.float32)
    y = xf * lax.rsqrt(jnp.mean(jnp.square(xf), axis=-1, keepdims=True) + RMS_EPS)
    return (y * g.astype(jnp.float32)).astype(x.dtype)


def rope(x, cos, sin):
    half = x.shape[-1] // 2
    x1, x2 = x[..., :half], x[..., half:]
    return jnp.concatenate([x1 * cos - x2 * sin, x1 * sin + x2 * cos], axis=-1)


def chunk_band_attention(q, k, v, rel_bias):
    bsz, s, h, dh = q.shape
    nc = s // CHUNK
    nb = A_LEFT_CHUNKS + 1
    qc = q.reshape(bsz, nc, CHUNK, h, dh)
    pad = ((0, 0), (A_LEFT_CHUNKS, 0), (0, 0), (0, 0), (0, 0))
    kp = jnp.pad(k.reshape(bsz, nc, CHUNK, h, dh), pad)
    vp = jnp.pad(v.reshape(bsz, nc, CHUNK, h, dh), pad)
    band = jnp.arange(nc)[:, None] + jnp.arange(nb)[None, :]
    kb = kp[:, band].reshape(bsz, nc, nb * CHUNK, h, dh)
    vb = vp[:, band].reshape(bsz, nc, nb * CHUNK, h, dh)
    valid = jnp.repeat(band >= A_LEFT_CHUNKS, CHUNK, axis=1)
    t = jnp.arange(CHUNK)
    kpos = ((jnp.arange(nb)[:, None] - A_LEFT_CHUNKS) * CHUNK + t[None, :]).reshape(-1)
    rel = t[:, None] - kpos[None, :]
    ridx = jnp.clip(rel, -REL_CLIP_RIGHT, REL_CLIP_LEFT) + REL_CLIP_RIGHT
    bias = rel_bias[:, ridx].astype(jnp.float32)
    sc = jnp.einsum('bnqhd,bnkhd->bhnqk', qc, kb).astype(jnp.float32) * (dh ** -0.5)
    sc = sc + bias[None, :, None]
    sc = jnp.where(valid[None, None, :, None, :], sc, NEG)
    p = jax.nn.softmax(sc, axis=-1).astype(v.dtype)
    o = jnp.einsum('bhnqk,bnkhd->bnqhd', p, vb)
    return o.reshape(bsz, s, h * dh)


def chunk_causal_attention(q, k, v):
    bsz, s, h, dq = q.shape
    dv = v.shape[-1]
    nqb = s // Q_BLOCK
    qb = q.reshape(bsz, nqb, Q_BLOCK, h, dq).transpose(1, 0, 2, 3, 4)
    kchunk = jnp.arange(s) // CHUNK
    scale = dq ** -0.5

    def block(args):
        q_blk, i = args
        qchunk = (i * Q_BLOCK + jnp.arange(Q_BLOCK)) // CHUNK
        mask = kchunk[None, :] <= qchunk[:, None]
        sc = jnp.einsum('bqhd,bkhd->bhqk', q_blk, k).astype(jnp.float32) * scale
        sc = jnp.where(mask[None, None], sc, NEG)
        p = jax.nn.softmax(sc, axis=-1).astype(v.dtype)
        return jnp.einsum('bhqk,bkhd->bqhd', p, v)

    o = lax.map(block, (qb, jnp.arange(nqb)))
    return o.transpose(1, 0, 2, 3, 4).reshape(bsz, s, h * dv)


def peer(h, wq, keys, u_tab, v_tab):
    bsz, s, d = h.shape
    q = (h @ wq).reshape(bsz, s, PEER_HEADS, 2, PEER_HALF)
    sc = jnp.einsum('bshpc,hpnc->bshpn', q, keys).astype(jnp.float32)
    sv, si = lax.top_k(sc, PEER_TOPK)
    comb = (sv[..., 0, :, None] + sv[..., 1, None, :]).reshape(bsz, s, PEER_HEADS, PEER_TOPK * PEER_TOPK)
    fv, fi = lax.top_k(comb, PEER_TOPK)
    e1 = jnp.take_along_axis(si[..., 0, :], fi // PEER_TOPK, axis=-1)
    e2 = jnp.take_along_axis(si[..., 1, :], fi % PEER_TOPK, axis=-1)
    experts = e1 * N_KEYS + e2
    g = jax.nn.softmax(fv, axis=-1).astype(h.dtype)
    nblk = (bsz * s) // TOKEN_BLOCK
    hb = h.reshape(nblk, TOKEN_BLOCK, d)
    eb = experts.reshape(nblk, TOKEN_BLOCK, PEER_HEADS, PEER_TOPK)
    gb = g.reshape(nblk, TOKEN_BLOCK, PEER_HEADS, PEER_TOPK)

    def block(args):
        xt, et, gt = args
        a = jnp.einsum('td,thkd->thk', xt, u_tab[et])
        w = gt * jax.nn.gelu(a, approximate=False)
        return jnp.einsum('thk,thkd->td', w, v_tab[et])

    out = lax.map(block, (hb, eb, gb))
    return out.reshape(bsz, s, d)


def setup_inputs(seed: int = 0) -> dict:
    key = jax.random.key(seed)
    ks = jax.random.split(key, 24)
    f32 = jnp.float32
    L = DEPTH
    D = D_MODEL

    def nrm(k, shape, scale):
        return jax.random.normal(k, shape, f32) * scale

    x = nrm(ks[0], (BATCH, SEQ, D), 1.0)
    c = nrm(ks[1], (BATCH, D), 1.0)
    offset = jax.random.randint(ks[2], (BATCH, 1), 0, 4096, dtype=jnp.int32)
    positions = offset + jnp.arange(SEQ, dtype=jnp.int32)[None, :]
    return {
        "x": x,
        "c": c,
        "positions": positions,
        "w_ada": nrm(ks[3], (L, D, 6 * D), 0.5 * D ** -0.5),
        "b_ada": nrm(ks[4], (L, 6 * D), 0.02),
        "w_in": nrm(ks[5], (L, D, IN_COLS), D ** -0.5),
        "b_in": nrm(ks[6], (L, IN_COLS), 0.02),
        "rel_bias": nrm(ks[7], (L, A_HEADS, REL_SIZE), 0.5),
        "q_norm_g": 1.0 + nrm(ks[8], (L, Q_LORA), 0.02),
        "kv_norm_g": 1.0 + nrm(ks[9], (L, KV_LORA), 0.02),
        "w_uq": nrm(ks[10], (L, Q_LORA, B_HEADS * (QK_NOPE + QK_ROPE)), Q_LORA ** -0.5),
        "w_ukv": nrm(ks[11], (L, KV_LORA, B_HEADS * (QK_NOPE + V_HEAD)), KV_LORA ** -0.5),
        "w_pa": nrm(ks[12], (L, A_WIDTH, D), A_WIDTH ** -0.5),
        "w_pb": nrm(ks[13], (L, B_WIDTH, D), B_WIDTH ** -0.5),
        "w_o": nrm(ks[14], (L, D, D), DN_BETA * D ** -0.5),
        "ln1_g": 1.0 + nrm(ks[15], (L, D), 0.02),
        "ln1_b": nrm(ks[16], (L, D), 0.02),
        "peer_wq": nrm(ks[17], (L, D, PEER_HEADS * PEER_QDIM), D ** -0.5),
        "peer_keys": nrm(ks[18], (L, PEER_HEADS, 2, N_KEYS, PEER_HALF), PEER_HALF ** -0.5),
        "peer_u": nrm(ks[19], (L, N_EXPERTS, D), D ** -0.5),
        "peer_v": nrm(ks[20], (L, N_EXPERTS, D), DN_BETA),
        "ln2_g": 1.0 + nrm(ks[21], (L, D), 0.02),
        "ln2_b": nrm(ks[22], (L, D), 0.02),
    }


def reference(x, c, positions, w_ada, b_ada, w_in, b_in, rel_bias, q_norm_g, kv_norm_g, w_uq, w_ukv,
              w_pa, w_pb, w_o, ln1_g, ln1_b, peer_wq, peer_keys, peer_u, peer_v, ln2_g, ln2_b):
    bsz, s, d = x.shape
    split_at = [int(i) for i in np.cumsum(IN_SIZES)[:-1]]
    inv_freq = ROPE_THETA ** (-jnp.arange(0, QK_ROPE, 2, dtype=jnp.float32) / QK_ROPE)
    ang = positions.astype(jnp.float32)[..., None] * inv_freq
    cos = jnp.cos(ang).astype(x.dtype)[:, :, None, :]
    sin = jnp.sin(ang).astype(x.dtype)[:, :, None, :]
    cond = jax.nn.silu(c)
    for l in range(DEPTH):
        mod = (cond @ w_ada[l] + b_ada[l]).reshape(bsz, 6, d)
        sh1, sc1, g1 = mod[:, 0, None, :], mod[:, 1, None, :], mod[:, 2, None, :]
        sh2, sc2, g2 = mod[:, 3, None, :], mod[:, 4, None, :], mod[:, 5, None, :]

        h = x * (1.0 + sc1) + sh1
        z = h @ w_in[l] + b_in[l]
        qa, ka, va, cq, ckv, kr, gate_a, gate_b = jnp.split(z, split_at, axis=-1)

        a_shape = (bsz, s, A_HEADS, A_HEAD_DIM)
        ya = chunk_band_attention(qa.reshape(a_shape), ka.reshape(a_shape), va.reshape(a_shape), rel_bias[l])
        ya = ya @ w_pa[l]

        qb = (rms_norm(cq, q_norm_g[l]) @ w_uq[l]).reshape(bsz, s, B_HEADS, QK_NOPE + QK_ROPE)
        q_nope, q_rope = qb[..., :QK_NOPE], rope(qb[..., QK_NOPE:], cos, sin)
        kv = (rms_norm(ckv, kv_norm_g[l]) @ w_ukv[l]).reshape(bsz, s, B_HEADS, QK_NOPE + V_HEAD)
        k_nope, vb = kv[..., :QK_NOPE], kv[..., QK_NOPE:]
        k_rope = jnp.broadcast_to(rope(kr[:, :, None, :], cos, sin), (bsz, s, B_HEADS, QK_ROPE))
        yb = chunk_causal_attention(jnp.concatenate([q_nope, q_rope], axis=-1),
                                    jnp.concatenate([k_nope, k_rope], axis=-1), vb)
        yb = yb @ w_pb[l]

        y = (jax.nn.sigmoid(gate_a) * ya + jax.nn.sigmoid(gate_b) * yb) @ w_o[l]
        x = layer_norm(DN_ALPHA * x + g1 * y, ln1_g[l], ln1_b[l])

        h2 = x * (1.0 + sc2) + sh2
        p = peer(h2, peer_wq[l], peer_keys[l], peer_u[l], peer_v[l])
        x = layer_norm(DN_ALPHA * x + g2 * p, ln2_g[l], ln2_b[l])
    return x
```

```python
import functools
import math

import jax
import jax.numpy as jnp
import numpy as np
from jax import lax
from jax.experimental import pallas as pl
from jax.experimental.pallas import tpu as pltpu

F32 = jnp.float32
BF16 = jnp.bfloat16

CHUNK = 64
A_HEADS = 8
A_HEAD_DIM = 128
A_LEFT_CHUNKS = 8
REL_CLIP_LEFT = 256
REL_CLIP_RIGHT = CHUNK - 1
B_HEADS = 8
Q_LORA = 512
KV_LORA = 512
QK_NOPE = 128
QK_ROPE = 64
V_HEAD = 128
ROPE_THETA = 10000.0
N_KEYS = 128
PEER_HEADS = 8
PEER_TOPK = 16
PEER_HALF = 128
DEPTH = 1
DN_ALPHA = (2 * DEPTH) ** 0.25
LN_EPS = 1e-5
RMS_EPS = 1e-6
NEG = -1e30

A_WIDTH = A_HEADS * A_HEAD_DIM
B_WIDTH = B_HEADS * V_HEAD
A_SCALE = A_HEAD_DIM ** -0.5
B_SCALE = (QK_NOPE + QK_ROPE) ** -0.5
SQRT_HALF = math.sqrt(0.5)

LANES = 128
SUBLANES_BF16 = 16
VMEM_LIMIT_BYTES = 56 * 1024 * 1024

IN_TN = 512
GA0 = 0
GB0 = 2048
QA0 = 4096
KA0 = 5120
VA0 = 6144
CQ0 = 7168
CKV0 = 7680
KR0 = 8192
Z_WIDTH = 8704
B_HEAD_PAD = 256


def _nt_dot(a, b):
    return lax.dot_general(a, b, (((1,), (1,)), ((), ())), preferred_element_type=F32)


def _params(*sem):
    return pltpu.CompilerParams(dimension_semantics=sem, vmem_limit_bytes=VMEM_LIMIT_BYTES)


def _layer_norm(r, g, b):
    mu = jnp.mean(r, axis=-1, keepdims=True)
    d = r - mu
    var = jnp.mean(d * d, axis=-1, keepdims=True)
    return d * lax.rsqrt(var + LN_EPS) * g + b


def _ada_kernel(c_ref, w_ref, b_ref, o_ref):
    c = c_ref[...]
    cond = (c * jax.nn.sigmoid(c)).astype(BF16)
    o_ref[...] = jnp.dot(cond, w_ref[...].astype(BF16), preferred_element_type=F32) + b_ref[...]


def _ada_mod(c, w_ada, b_ada):
    bsz, d = c.shape
    n = w_ada.shape[1]
    tn = 1024
    return pl.pallas_call(
        _ada_kernel,
        grid=(n // tn,),
        in_specs=[pl.BlockSpec((bsz, d), lambda j: (0, 0)),
                  pl.BlockSpec((d, tn), lambda j: (0, j)),
                  pl.BlockSpec((1, tn), lambda j: (0, j))],
        out_specs=pl.BlockSpec((bsz, tn), lambda j: (0, j)),
        out_shape=jax.ShapeDtypeStruct((bsz, n), F32),
        compiler_params=_params("arbitrary"),
        name="ada_mod",
    )(c, w_ada, b_ada.reshape(1, n))


def _rope_kernel(pos_ref, invf_ref, cos_ref, sin_ref):
    ang = pos_ref[...] * invf_ref[...]
    cos_ref[...] = jnp.cos(ang)
    sin_ref[...] = jnp.sin(ang)


def _rope_tables(positions):
    t = positions.size
    half = QK_ROPE // 2
    per_row = LANES // half
    rows = t // per_row
    inv_freq = ROPE_THETA ** (-jnp.arange(0, QK_ROPE, 2, dtype=F32) / QK_ROPE)
    pos_rep = jnp.repeat(positions.astype(F32).reshape(rows, per_row), half, axis=1)
    invf = jnp.tile(inv_freq, per_row).reshape(1, LANES)
    tr = min(rows, 2048)
    cos, sin = pl.pallas_call(
        _rope_kernel,
        grid=(rows // tr,),
        in_specs=[pl.BlockSpec((tr, LANES), lambda i: (i, 0)),
                  pl.BlockSpec((1, LANES), lambda i: (0, 0))],
        out_specs=[pl.BlockSpec((tr, LANES), lambda i: (i, 0))] * 2,
        out_shape=[jax.ShapeDtypeStruct((rows, LANES), F32)] * 2,
        compiler_params=_params("arbitrary"),
        name="rope_tab",
    )(pos_rep, invf)
    cos = cos.reshape(t, half)
    sin = sin.reshape(t, half)
    zeros = jnp.zeros((t, LANES - QK_ROPE), F32)
    return (jnp.concatenate([cos, cos, zeros], axis=1),
            jnp.concatenate([-sin, sin, zeros], axis=1))


def _inproj_kernel(x_ref, sc_ref, sh_ref, w_ref, b_ref, gq_ref, gkv_ref, z_ref, h_scr):
    j = pl.program_id(1)

    @pl.when(j == 0)
    def _():
        h_scr[...] = (x_ref[...] * (1.0 + sc_ref[...]) + sh_ref[...]).astype(BF16)

    acc = jnp.dot(h_scr[...], w_ref[...], preferred_element_type=F32) + b_ref[...]

    def rms(g):
        ms = jnp.mean(acc * acc, axis=-1, keepdims=True)
        return acc * lax.rsqrt(ms + RMS_EPS) * g

    @pl.when(j < QA0 // IN_TN)
    def _():
        z_ref[...] = jax.nn.sigmoid(acc).astype(BF16)

    @pl.when(jnp.logical_and(j >= QA0 // IN_TN, j < KA0 // IN_TN))
    def _():
        z_ref[...] = (acc * A_SCALE).astype(BF16)

    @pl.when(jnp.logical_or(jnp.logical_and(j >= KA0 // IN_TN, j < CQ0 // IN_TN), j == KR0 // IN_TN))
    def _():
        z_ref[...] = acc.astype(BF16)

    @pl.when(j == CQ0 // IN_TN)
    def _():
        z_ref[...] = rms(gq_ref[...]).astype(BF16)

    @pl.when(j == CKV0 // IN_TN)
    def _():
        z_ref[...] = rms(gkv_ref[...]).astype(BF16)


def _in_proj(x2, mod4, w2, b2, gq, gkv, seq):
    t, d = x2.shape
    tm = min(1024, seq)
    tpb = seq // tm
    return pl.pallas_call(
        _inproj_kernel,
        grid=(t // tm, Z_WIDTH // IN_TN),
        in_specs=[pl.BlockSpec((tm, d), lambda i, j: (i, 0)),
                  pl.BlockSpec((None, None, 1, d), lambda i, j: (i // tpb, 1, 0, 0)),
                  pl.BlockSpec((None, None, 1, d), lambda i, j: (i // tpb, 0, 0, 0)),
                  pl.BlockSpec((d, IN_TN), lambda i, j: (0, j)),
                  pl.BlockSpec((1, IN_TN), lambda i, j: (0, j)),
                  pl.BlockSpec((1, Q_LORA), lambda i, j: (0, 0)),
                  pl.BlockSpec((1, KV_LORA), lambda i, j: (0, 0))],
        out_specs=pl.BlockSpec((tm, IN_TN), lambda i, j: (i, j)),
        out_shape=jax.ShapeDtypeStruct((t, Z_WIDTH), BF16),
        scratch_shapes=[pltpu.VMEM((tm, d), BF16)],
        compiler_params=_params("arbitrary", "arbitrary"),
        name="in_proj",
    )(x2, mod4, mod4, w2, b2, gq, gkv)


def _mla_up_kernel(cq_ref, ckv_ref, kra_ref, krb_ref, cos_ref, sin_ref,
                   wq1_ref, wq2_ref, wk_ref, wv_ref, q_ref, k_ref, v_ref):
    cq = cq_ref[...]
    ckv = ckv_ref[...]
    cos = cos_ref[...]
    sin = sin_ref[...]
    qa = jnp.dot(cq, wq1_ref[...], preferred_element_type=F32)
    qb = jnp.dot(cq, wq2_ref[...], preferred_element_type=F32)
    kn = jnp.dot(ckv, wk_ref[...], preferred_element_type=F32)
    v_ref[...] = jnp.dot(ckv, wv_ref[...], preferred_element_type=F32).astype(BF16)
    k_rope = (kra_ref[...].astype(F32) * cos + krb_ref[...].astype(F32) * sin).astype(BF16)
    for h in range(B_HEADS):
        c0 = h * B_HEAD_PAD
        n0 = h * QK_NOPE
        q_ref[:, c0:c0 + QK_NOPE] = (qa[:, c0:c0 + QK_NOPE] * B_SCALE).astype(BF16)
        q_rope = qa[:, c0 + QK_NOPE:c0 + B_HEAD_PAD] * cos + qb[:, n0:n0 + QK_NOPE] * sin
        q_ref[:, c0 + QK_NOPE:c0 + B_HEAD_PAD] = (q_rope * B_SCALE).astype(BF16)
        k_ref[:, c0:c0 + QK_NOPE] = kn[:, n0:n0 + QK_NOPE].astype(BF16)
        k_ref[:, c0 + QK_NOPE:c0 + B_HEAD_PAD] = k_rope


def _mla_up(z, cos_t, sin_t, wq1, wq2, wk, wv):
    t = z.shape[0]
    tm = 512
    const = lambda i: (0, 0)
    return pl.pallas_call(
        _mla_up_kernel,
        grid=(t // tm,),
        in_specs=[pl.BlockSpec((tm, Q_LORA), lambda i: (i, CQ0 // Q_LORA)),
                  pl.BlockSpec((tm, KV_LORA), lambda i: (i, CKV0 // KV_LORA)),
                  pl.BlockSpec((tm, LANES), lambda i: (i, KR0 // LANES)),
                  pl.BlockSpec((tm, LANES), lambda i: (i, KR0 // LANES + 1)),
                  pl.BlockSpec((tm, LANES), lambda i: (i, 0)),
                  pl.BlockSpec((tm, LANES), lambda i: (i, 0)),
                  pl.BlockSpec(wq1.shape, const),
                  pl.BlockSpec(wq2.shape, const),
                  pl.BlockSpec(wk.shape, const),
                  pl.BlockSpec(wv.shape, const)],
        out_specs=[pl.BlockSpec((tm, B_HEADS * B_HEAD_PAD), lambda i: (i, 0)),
                   pl.BlockSpec((tm, B_HEADS * B_HEAD_PAD), lambda i: (i, 0)),
                   pl.BlockSpec((tm, B_WIDTH), lambda i: (i, 0))],
        out_shape=[jax.ShapeDtypeStruct((t, B_HEADS * B_HEAD_PAD), BF16),
                   jax.ShapeDtypeStruct((t, B_HEADS * B_HEAD_PAD), BF16),
                   jax.ShapeDtypeStruct((t, B_WIDTH), BF16)],
        compiler_params=_params("arbitrary"),
        name="mla_up",
    )(z, z, z, z, cos_t, sin_t, wq1, wq2, wk, wv)


def _flash_kernel(qi_ref, kj_ref, q_ref, k_ref, v_ref, o_ref, m_scr, l_scr, acc_scr):
    p = pl.program_id(2)
    i = qi_ref[p]
    j = kj_ref[p]
    tq = q_ref.shape[0]
    tk = k_ref.shape[0]

    @pl.when(j == 0)
    def _():
        m_scr[...] = jnp.full(m_scr.shape, -jnp.inf, F32)
        l_scr[...] = jnp.zeros(l_scr.shape, F32)
        acc_scr[...] = jnp.zeros(acc_scr.shape, F32)

    def update(s):
        m_prev = m_scr[...]
        m_new = jnp.maximum(m_prev, jnp.max(s, axis=1, keepdims=True))
        alpha = jnp.exp(m_prev - m_new)
        pexp = jnp.exp(s - m_new)
        l_scr[...] = alpha * l_scr[...] + jnp.sum(pexp, axis=1, keepdims=True)
        acc_scr[...] = alpha * acc_scr[...] + jnp.dot(pexp.astype(BF16), v_ref[...],
                                                      preferred_element_type=F32)
        m_scr[...] = m_new

    @pl.when(j < i)
    def _():
        update(_nt_dot(q_ref[...], k_ref[...]))

    @pl.when(j == i)
    def _():
        s = _nt_dot(q_ref[...], k_ref[...])
        qpos = lax.broadcasted_iota(jnp.int32, (tq, tk), 0)
        kpos = lax.broadcasted_iota(jnp.int32, (tq, tk), 1)
        update(jnp.where(kpos <= (qpos | (CHUNK - 1)), s, NEG))
        o_ref[...] = (acc_scr[...] / l_scr[...]).astype(BF16)


def _mla_flash(q, k, v, bsz, seq):
    tq = min(1024, seq)
    nqb = seq // tq
    pairs = [(i, j) for i in range(nqb) for j in range(i + 1)]
    qi = jnp.asarray([p[0] for p in pairs], jnp.int32)
    kj = jnp.asarray([p[1] for p in pairs], jnp.int32)
    grid_spec = pltpu.PrefetchScalarGridSpec(
        num_scalar_prefetch=2,
        grid=(bsz, B_HEADS, len(pairs)),
        in_specs=[pl.BlockSpec((tq, B_HEAD_PAD), lambda b, h, p, qi, kj: (b * nqb + qi[p], h)),
                  pl.BlockSpec((tq, B_HEAD_PAD), lambda b, h, p, qi, kj: (b * nqb + kj[p], h)),
                  pl.BlockSpec((tq, V_HEAD), lambda b, h, p, qi, kj: (b * nqb + kj[p], h))],
        out_specs=pl.BlockSpec((tq, V_HEAD), lambda b, h, p, qi, kj: (b * nqb + qi[p], h)),
        scratch_shapes=[pltpu.VMEM((tq, 1), F32), pltpu.VMEM((tq, 1), F32),
                        pltpu.VMEM((tq, V_HEAD), F32)],
    )
    return pl.pallas_call(
        _flash_kernel,
        grid_spec=grid_spec,
        out_shape=jax.ShapeDtypeStruct((bsz * seq, B_WIDTH), BF16),
        compiler_params=_params("arbitrary", "arbitrary", "arbitrary"),
        name="mla_flash",
    )(qi, kj, q, k, v)


BAND_TQ = A_LEFT_CHUNKS * CHUNK


def _band_kernel(q_ref, kp_ref, kc_ref, vp_ref, vc_ref, bias_ref, o_ref):
    i = pl.program_id(2)
    q = q_ref[...]
    first = jnp.where(i == 0, NEG, 0.0).astype(F32)
    s_prev = _nt_dot(q, kp_ref[...]) + bias_ref[:, :BAND_TQ] + first
    s_cur = _nt_dot(q, kc_ref[...]) + bias_ref[:, BAND_TQ:]
    m = jnp.maximum(jnp.max(s_prev, axis=1, keepdims=True), jnp.max(s_cur, axis=1, keepdims=True))
    p_prev = jnp.exp(s_prev - m)
    p_cur = jnp.exp(s_cur - m)
    l = jnp.sum(p_prev, axis=1, keepdims=True) + jnp.sum(p_cur, axis=1, keepdims=True)
    o = (jnp.dot(p_prev.astype(BF16), vp_ref[...], preferred_element_type=F32)
         + jnp.dot(p_cur.astype(BF16), vc_ref[...], preferred_element_type=F32))
    o_ref[...] = (o / l).astype(BF16)


def _band_bias(rel_bias):
    ql = jnp.arange(BAND_TQ)[:, None] + BAND_TQ
    kl = jnp.arange(2 * BAND_TQ)[None, :]
    ridx = jnp.clip(ql - kl, -REL_CLIP_RIGHT, REL_CLIP_LEFT) + REL_CLIP_RIGHT
    qc = ql // CHUNK
    kc = kl // CHUNK
    valid = jnp.logical_and(kc <= qc, kc >= qc - A_LEFT_CHUNKS)
    return jnp.where(valid[None], rel_bias[:, ridx].astype(F32), NEG)


def _band_attn(z, bias, bsz, seq):
    nq = seq // BAND_TQ
    qb, kb, vb = QA0 // A_HEAD_DIM, KA0 // A_HEAD_DIM, VA0 // A_HEAD_DIM
    cur = lambda c0: (lambda h, b, i: (b * nq + i, c0 + h))
    prev = lambda c0: (lambda h, b, i: (b * nq + jnp.maximum(i - 1, 0), c0 + h))
    blk = (BAND_TQ, A_HEAD_DIM)
    return pl.pallas_call(
        _band_kernel,
        grid=(A_HEADS, bsz, nq),
        in_specs=[pl.BlockSpec(blk, cur(qb)),
                  pl.BlockSpec(blk, prev(kb)), pl.BlockSpec(blk, cur(kb)),
                  pl.BlockSpec(blk, prev(vb)), pl.BlockSpec(blk, cur(vb)),
                  pl.BlockSpec((None, BAND_TQ, 2 * BAND_TQ), lambda h, b, i: (h, 0, 0))],
        out_specs=pl.BlockSpec(blk, lambda h, b, i: (b * nq + i, h)),
        out_shape=jax.ShapeDtypeStruct((bsz * seq, A_WIDTH), BF16),
        compiler_params=_params("arbitrary", "arbitrary", "arbitrary"),
        name="band_attn",
    )(z, z, z, z, z, bias)


def _outproj_kernel(ya_ref, yb_ref, ga_ref, gb_ref, x_ref, g1_ref, sc2_ref, sh2_ref,
                    wpa_ref, wpb_ref, wo_ref, lg_ref, lb_ref, x1_ref, h2_ref):
    ta = jnp.dot(ya_ref[...], wpa_ref[...], preferred_element_type=F32)
    tb = jnp.dot(yb_ref[...], wpb_ref[...], preferred_element_type=F32)
    u = ga_ref[...].astype(F32) * ta + gb_ref[...].astype(F32) * tb
    y = jnp.dot(u.astype(BF16), wo_ref[...], preferred_element_type=F32)
    x1 = _layer_norm(DN_ALPHA * x_ref[...] + g1_ref[...] * y, lg_ref[...], lb_ref[...])
    x1_ref[...] = x1
    h2_ref[...] = (x1 * (1.0 + sc2_ref[...]) + sh2_ref[...]).astype(BF16)


def _out_proj(ya, yb, z, x2, mod4, wpa, wpb, wo, lg, lb, seq):
    t, d = x2.shape
    tm = 256
    tpb = seq // tm
    const = lambda i: (0, 0)
    modspec = lambda k: pl.BlockSpec((None, None, 1, d), lambda i: (i // tpb, k, 0, 0))
    return pl.pallas_call(
        _outproj_kernel,
        grid=(t // tm,),
        in_specs=[pl.BlockSpec((tm, A_WIDTH), lambda i: (i, 0)),
                  pl.BlockSpec((tm, B_WIDTH), lambda i: (i, 0)),
                  pl.BlockSpec((tm, d), lambda i: (i, GA0 // d)),
                  pl.BlockSpec((tm, d), lambda i: (i, GB0 // d)),
                  pl.BlockSpec((tm, d), lambda i: (i, 0)),
                  modspec(2), modspec(4), modspec(3),
                  pl.BlockSpec(wpa.shape, const), pl.BlockSpec(wpb.shape, const),
                  pl.BlockSpec(wo.shape, const),
                  pl.BlockSpec((1, d), const), pl.BlockSpec((1, d), const)],
        out_specs=[pl.BlockSpec((tm, d), lambda i: (i, 0)),
                   pl.BlockSpec((tm, d), lambda i: (i, 0))],
        out_shape=[jax.ShapeDtypeStruct((t, d), F32), jax.ShapeDtypeStruct((t, d), BF16)],
        compiler_params=_params("arbitrary"),
        name="out_proj",
    )(ya, yb, z, z, x2, mod4, mod4, mod4, wpa, wpb, wo, lg, lb)


TOPV = PEER_TOPK + 1
TOPV_ROWS = 24
COMB_DENSE = 8
COMB_ROWS = COMB_DENSE * PEER_TOPK + 2 * (TOPV_ROWS - COMB_DENSE)


def _top_vals(s, out_scr):
    for r in range(TOPV):
        m = jnp.max(s, axis=0, keepdims=True)
        out_scr[r:r + 1, :] = m
        if r + 1 < TOPV:
            s = jnp.where(s == m, -jnp.inf, s)


def _topk_kernel(h2_ref, wqt_ref, keys_ref, tau_ref, e1_ref, s2_ref, e2_ref,
                 st_scr, tv1_scr, tv2_scr, comb_scr, fv_scr):
    pqt = _nt_dot(wqt_ref[...], h2_ref[...]).astype(BF16)
    for hp in range(2 * PEER_HEADS):
        r0 = hp * PEER_HALF
        st_scr[r0:r0 + N_KEYS, :] = jnp.dot(keys_ref[hp], pqt[r0:r0 + PEER_HALF, :],
                                            preferred_element_type=F32)
    pad = jnp.full((TOPV_ROWS - PEER_TOPK, h2_ref.shape[0]), -jnp.inf, F32)
    tv1_scr[PEER_TOPK:, :] = pad
    tv2_scr[PEER_TOPK:, :] = pad

    def head(h, carry):
        r1 = pl.multiple_of(h * 2 * N_KEYS, 2 * N_KEYS)
        s1 = st_scr[pl.ds(r1, N_KEYS), :]
        s2 = st_scr[pl.ds(r1 + N_KEYS, N_KEYS), :]
        _top_vals(s1, tv1_scr)
        _top_vals(s2, tv2_scr)
        tv2 = tv2_scr[0:PEER_TOPK, :]
        for i in range(COMB_DENSE):
            comb_scr[i * PEER_TOPK:(i + 1) * PEER_TOPK, :] = tv1_scr[i:i + 1, :] + tv2
        c0 = COMB_DENSE * PEER_TOPK
        c1 = c0 + TOPV_ROWS - COMB_DENSE
        comb_scr[c0:c1, :] = tv1_scr[COMB_DENSE:, :] + tv2_scr[0:1, :]
        comb_scr[c1:, :] = tv2_scr[COMB_DENSE:, :] + tv1_scr[0:1, :]
        _top_vals(comb_scr[...], fv_scr)
        top = fv_scr[0:1, :]
        kept = fv_scr[0:PEER_TOPK, :]
        theta = 0.5 * (fv_scr[PEER_TOPK - 1:PEER_TOPK, :] + fv_scr[PEER_TOPK:PEER_TOPK + 1, :])
        inv_z = 1.0 / jnp.sum(jnp.exp(kept - top), axis=0, keepdims=True)
        tau_ref[h] = theta - s1
        e1_ref[h] = jnp.exp(s1 - tv1_scr[0:1, :]) * inv_z
        s2_ref[h] = s2
        e2_ref[h] = jnp.exp(s2 - tv2_scr[0:1, :])
        return carry

    lax.fori_loop(0, PEER_HEADS, head, 0)


def _peer_topk(h2, wqt, keys_b):
    t, d = h2.shape
    tm = 512
    rows = PEER_HEADS * 2 * N_KEYS
    out = jax.ShapeDtypeStruct((PEER_HEADS, N_KEYS, t), F32)
    ospec = pl.BlockSpec((PEER_HEADS, N_KEYS, tm), lambda i: (0, 0, i))
    return pl.pallas_call(
        _topk_kernel,
        grid=(t // tm,),
        in_specs=[pl.BlockSpec((tm, d), lambda i: (i, 0)),
                  pl.BlockSpec(wqt.shape, lambda i: (0, 0)),
                  pl.BlockSpec(keys_b.shape, lambda i: (0, 0, 0))],
        out_specs=[ospec] * 4,
        out_shape=[out] * 4,
        scratch_shapes=[pltpu.VMEM((rows, tm), F32),
                        pltpu.VMEM((TOPV_ROWS, tm), F32),
                        pltpu.VMEM((TOPV_ROWS, tm), F32),
                        pltpu.VMEM((COMB_ROWS, tm), F32),
                        pltpu.VMEM((TOPV_ROWS, tm), F32)],
        compiler_params=_params("arbitrary"),
        name="peer_topk",
    )(h2, wqt, keys_b)


PEER_TE = 1024
PEER_TM = 512


def _dense_kernel(h2_ref, u_ref, vt_ref, tau_ref, e1_ref, s2_ref, e2_ref, pt_ref, at_scr, wt_scr):
    e = pl.program_id(1)

    @pl.when(e == 0)
    def _():
        pt_ref[...] = jnp.zeros(pt_ref.shape, F32)

    at_scr[...] = _nt_dot(u_ref[...], h2_ref[...])
    rows = SUBLANES_BF16
    for lb in range(PEER_TM // LANES):
        lanes = slice(lb * LANES, (lb + 1) * LANES)
        for g in range(N_KEYS // rows):
            krows = slice(g * rows, (g + 1) * rows)
            s2 = [s2_ref[h, krows, lanes] for h in range(PEER_HEADS)]
            e2 = [e2_ref[h, krows, lanes] for h in range(PEER_HEADS)]
            for k1 in range(PEER_TE // N_KEYS):
                gate = jnp.zeros((rows, LANES), F32)
                for h in range(PEER_HEADS):
                    tau = tau_ref[h, k1:k1 + 1, lanes]
                    e1 = e1_ref[h, k1:k1 + 1, lanes]
                    gate = gate + jnp.where(s2[h] >= tau, e2[h], 0.0) * e1
                r0 = k1 * N_KEYS + g * rows
                a = at_scr[r0:r0 + rows, lanes]
                w = 0.5 * a * (1.0 + lax.erf(a * SQRT_HALF)) * gate
                wt_scr[r0:r0 + rows, lanes] = w.astype(BF16)
    pt_ref[...] += jnp.dot(vt_ref[...], wt_scr[...], preferred_element_type=F32)


def _peer_dense(h2, u_b, vt_b, tau, e1, s2, e2):
    t, d = h2.shape
    n_exp = u_b.shape[0]
    k1_blk = PEER_TE // N_KEYS
    rspec = pl.BlockSpec((PEER_HEADS, k1_blk, PEER_TM), lambda i, e: (0, e, i))
    fspec = pl.BlockSpec((PEER_HEADS, N_KEYS, PEER_TM), lambda i, e: (0, 0, i))
    return pl.pallas_call(
        _dense_kernel,
        grid=(t // PEER_TM, n_exp // PEER_TE),
        in_specs=[pl.BlockSpec((PEER_TM, d), lambda i, e: (i, 0)),
                  pl.BlockSpec((PEER_TE, d), lambda i, e: (e, 0)),
                  pl.BlockSpec((d, PEER_TE), lambda i, e: (0, e)),
                  rspec, rspec, fspec, fspec],
        out_specs=pl.BlockSpec((d, PEER_TM), lambda i, e: (0, i)),
        out_shape=jax.ShapeDtypeStruct((d, t), F32),
        scratch_shapes=[pltpu.VMEM((PEER_TE, PEER_TM), F32),
                        pltpu.VMEM((PEER_TE, PEER_TM), BF16)],
        compiler_params=_params("arbitrary", "arbitrary"),
        name="peer_dense",
    )(h2, u_b, vt_b, tau, e1, s2, e2)


def _final_kernel(pt_ref, x1_ref, g2_ref, lg_ref, lb_ref, o_ref):
    p = pt_ref[...].T
    o_ref[...] = _layer_norm(DN_ALPHA * x1_ref[...] + g2_ref[...] * p, lg_ref[...], lb_ref[...])


def _peer_final(pt, x1, mod4, lg, lb, seq):
    t, d = x1.shape
    tm = 512
    tpb = seq // tm
    return pl.pallas_call(
        _final_kernel,
        grid=(t // tm,),
        in_specs=[pl.BlockSpec((d, tm), lambda i: (0, i)),
                  pl.BlockSpec((tm, d), lambda i: (i, 0)),
                  pl.BlockSpec((None, None, 1, d), lambda i: (i // tpb, 5, 0, 0)),
                  pl.BlockSpec((1, d), lambda i: (0, 0)),
                  pl.BlockSpec((1, d), lambda i: (0, 0))],
        out_specs=pl.BlockSpec((tm, d), lambda i: (i, 0)),
        out_shape=jax.ShapeDtypeStruct((t, d), F32),
        compiler_params=_params("arbitrary"),
        name="peer_final",
    )(pt, x1, mod4, lg, lb)


def _half_swap(w):
    half = w.shape[-1] // 2
    return jnp.concatenate([w[..., half:], w[..., :half]], axis=-1)


def _in_proj_layout(w, pad_to):
    o = np.cumsum([0, A_WIDTH, A_WIDTH, A_WIDTH, Q_LORA, KV_LORA, QK_ROPE, pad_to, pad_to])
    qa, ka, va, cq, ckv, kr, ga, gb = [w[..., o[n]:o[n + 1]] for n in range(8)]
    z64 = jnp.zeros(kr.shape[:-1] + (LANES - QK_ROPE,), w.dtype)
    ztail = jnp.zeros(kr.shape[:-1] + (IN_TN - 2 * LANES,), w.dtype)
    return jnp.concatenate([ga, gb, qa, ka, va, cq, ckv, kr, z64, _half_swap(kr), z64, ztail], axis=-1)


def _mla_weight_layout(w_uq, w_ukv):
    r = w_uq.shape[0]
    uq = w_uq.reshape(r, B_HEADS, QK_NOPE + QK_ROPE)
    nope, rope = uq[..., :QK_NOPE], uq[..., QK_NOPE:]
    z64 = jnp.zeros((r, B_HEADS, B_HEAD_PAD - QK_NOPE - QK_ROPE), w_uq.dtype)
    wq1 = jnp.concatenate([nope, rope, z64], axis=-1).reshape(r, B_HEADS * B_HEAD_PAD)
    wq2 = jnp.concatenate([_half_swap(rope), z64], axis=-1).reshape(r, B_HEADS * LANES)
    ukv = w_ukv.reshape(w_ukv.shape[0], B_HEADS, QK_NOPE + V_HEAD)
    wk = ukv[..., :QK_NOPE].reshape(-1, B_HEADS * QK_NOPE)
    wv = ukv[..., QK_NOPE:].reshape(-1, B_WIDTH)
    return wq1.astype(BF16), wq2.astype(BF16), wk.astype(BF16), wv.astype(BF16)


def kernel(x, c, positions, w_ada, b_ada, w_in, b_in, rel_bias, q_norm_g, kv_norm_g, w_uq, w_ukv,
           w_pa, w_pb, w_o, ln1_g, ln1_b, peer_wq, peer_keys, peer_u, peer_v, ln2_g, ln2_b):
    bsz, seq, d = x.shape
    t = bsz * seq
    assert d == GB0 - GA0 and seq % 1024 == 0 and w_ada.shape[0] == 1
    x2 = x.reshape(t, d)
    row = lambda a: a.reshape(1, -1).astype(F32)

    mod4 = _ada_mod(c, w_ada[0], b_ada[0]).reshape(bsz, 6, 1, d)
    cos_t, sin_t = _rope_tables(positions)

    w2 = _in_proj_layout(w_in[0], d).astype(BF16)
    b2 = _in_proj_layout(b_in[0], d).reshape(1, Z_WIDTH).astype(F32)
    z = _in_proj(x2, mod4, w2, b2, row(q_norm_g[0]), row(kv_norm_g[0]), seq)

    wq1, wq2, wk, wv = _mla_weight_layout(w_uq[0], w_ukv[0])
    qb, kb, vb = _mla_up(z, cos_t, sin_t, wq1, wq2, wk, wv)
    yb = _mla_flash(qb, kb, vb, bsz, seq)
    ya = _band_attn(z, _band_bias(rel_bias[0]), bsz, seq)

    x1, h2 = _out_proj(ya, yb, z, x2, mod4, w_pa[0].astype(BF16), w_pb[0].astype(BF16),
                       w_o[0].astype(BF16), row(ln1_g[0]), row(ln1_b[0]), seq)

    wqt = peer_wq[0].astype(BF16).T
    keys_b = peer_keys[0].reshape(2 * PEER_HEADS, N_KEYS, PEER_HALF).astype(BF16)
    tau, e1, s2, e2 = _peer_topk(h2, wqt, keys_b)
    pt = _peer_dense(h2, peer_u[0].astype(BF16), peer_v[0].astype(BF16).T, tau, e1, s2, e2)
    out = _peer_final(pt, x1, mod4, row(ln2_g[0]), row(ln2_b[0]), seq)
    return out.reshape(bsz, seq, d)
```

```python
import functools
import math

import jax
import jax.numpy as jnp
import numpy as np
from jax import lax
from jax.experimental import pallas as pl
from jax.experimental.pallas import tpu as pltpu

F32 = jnp.float32
BF16 = jnp.bfloat16

CHUNK = 64
A_HEADS = 8
A_HEAD_DIM = 128
A_LEFT_CHUNKS = 8
REL_CLIP_LEFT = 256
REL_CLIP_RIGHT = CHUNK - 1
B_HEADS = 8
Q_LORA = 512
KV_LORA = 512
QK_NOPE = 128
QK_ROPE = 64
V_HEAD = 128
ROPE_THETA = 10000.0
N_KEYS = 128
PEER_HEADS = 8
PEER_TOPK = 16
PEER_HALF = 128
DEPTH = 1
DN_ALPHA = (2 * DEPTH) ** 0.25
LN_EPS = 1e-5
RMS_EPS = 1e-6
NEG = -1e30

A_WIDTH = A_HEADS * A_HEAD_DIM
B_WIDTH = B_HEADS * V_HEAD
A_SCALE = A_HEAD_DIM ** -0.5
B_SCALE = (QK_NOPE + QK_ROPE) ** -0.5
SQRT_HALF = math.sqrt(0.5)

LANES = 128
SUBLANES_BF16 = 16
VMEM_LIMIT_BYTES = 56 * 1024 * 1024

IN_TN = 512
GA0 = 0
GB0 = 2048
QA0 = 4096
KA0 = 5120
VA0 = 6144
CQ0 = 7168
CKV0 = 7680
KR0 = 8192
Z_WIDTH = 8704
B_HEAD_PAD = 256


def _nt_dot(a, b):
    return lax.dot_general(a, b, (((1,), (1,)), ((), ())), preferred_element_type=F32)


def _params(*sem, flags=None):
    return pltpu.CompilerParams(dimension_semantics=sem, vmem_limit_bytes=VMEM_LIMIT_BYTES, flags=flags)


def _layer_norm(r, g, b):
    mu = jnp.mean(r, axis=-1, keepdims=True)
    d = r - mu
    var = jnp.mean(d * d, axis=-1, keepdims=True)
    return d * lax.rsqrt(var + LN_EPS) * g + b


def _ada_kernel(c_ref, w_ref, b_ref, o_ref):
    c = c_ref[...]
    cond = (c * jax.nn.sigmoid(c)).astype(BF16)
    o_ref[...] = jnp.dot(cond, w_ref[...].astype(BF16), preferred_element_type=F32) + b_ref[...]


def _ada_mod(c, w_ada, b_ada):
    bsz, d = c.shape
    n = w_ada.shape[1]
    tn = 1024
    return pl.pallas_call(
        _ada_kernel,
        grid=(n // tn,),
        in_specs=[pl.BlockSpec((bsz, d), lambda j: (0, 0)),
                  pl.BlockSpec((d, tn), lambda j: (0, j)),
                  pl.BlockSpec((1, tn), lambda j: (0, j))],
        out_specs=pl.BlockSpec((bsz, tn), lambda j: (0, j)),
        out_shape=jax.ShapeDtypeStruct((bsz, n), F32),
        compiler_params=_params("arbitrary"),
        name="ada_mod",
    )(c, w_ada, b_ada.reshape(1, n))


def _rope_kernel(pos_ref, invf_ref, cos_ref, sin_ref):
    ang = pos_ref[...] * invf_ref[...]
    cos_ref[...] = jnp.cos(ang)
    sin_ref[...] = jnp.sin(ang)


def _rope_tables(positions):
    t = positions.size
    half = QK_ROPE // 2
    per_row = LANES // half
    rows = t // per_row
    inv_freq = ROPE_THETA ** (-jnp.arange(0, QK_ROPE, 2, dtype=F32) / QK_ROPE)
    pos_rep = jnp.repeat(positions.astype(F32).reshape(rows, per_row), half, axis=1)
    invf = jnp.tile(inv_freq, per_row).reshape(1, LANES)
    tr = min(rows, 2048)
    cos, sin = pl.pallas_call(
        _rope_kernel,
        grid=(rows // tr,),
        in_specs=[pl.BlockSpec((tr, LANES), lambda i: (i, 0)),
                  pl.BlockSpec((1, LANES), lambda i: (0, 0))],
        out_specs=[pl.BlockSpec((tr, LANES), lambda i: (i, 0))] * 2,
        out_shape=[jax.ShapeDtypeStruct((rows, LANES), F32)] * 2,
        compiler_params=_params("arbitrary"),
        name="rope_tab",
    )(pos_rep, invf)
    cos = cos.reshape(t, half)
    sin = sin.reshape(t, half)
    zeros = jnp.zeros((t, LANES - QK_ROPE), F32)
    return (jnp.concatenate([cos, cos, zeros], axis=1),
            jnp.concatenate([-sin, sin, zeros], axis=1))


def _inproj_kernel(x_ref, sc_ref, sh_ref, w_ref, b_ref, gq_ref, gkv_ref, z_ref, h_scr):
    j = pl.program_id(1)

    @pl.when(j == 0)
    def _():
        h_scr[...] = (x_ref[...] * (1.0 + sc_ref[...]) + sh_ref[...]).astype(BF16)

    acc = jnp.dot(h_scr[...], w_ref[...], preferred_element_type=F32) + b_ref[...]

    def rms(g):
        ms = jnp.mean(acc * acc, axis=-1, keepdims=True)
        return acc * lax.rsqrt(ms + RMS_EPS) * g

    @pl.when(j < QA0 // IN_TN)
    def _():
        z_ref[...] = jax.nn.sigmoid(acc).astype(BF16)

    @pl.when(jnp.logical_and(j >= QA0 // IN_TN, j < KA0 // IN_TN))
    def _():
        z_ref[...] = (acc * A_SCALE).astype(BF16)

    @pl.when(jnp.logical_or(jnp.logical_and(j >= KA0 // IN_TN, j < CQ0 // IN_TN), j == KR0 // IN_TN))
    def _():
        z_ref[...] = acc.astype(BF16)

    @pl.when(j == CQ0 // IN_TN)
    def _():
        z_ref[...] = rms(gq_ref[...]).astype(BF16)

    @pl.when(j == CKV0 // IN_TN)
    def _():
        z_ref[...] = rms(gkv_ref[...]).astype(BF16)


def _in_proj(x2, mod4, w2, b2, gq, gkv, seq):
    t, d = x2.shape
    tm = min(1024, seq)
    tpb = seq // tm
    return pl.pallas_call(
        _inproj_kernel,
        grid=(t // tm, Z_WIDTH // IN_TN),
        in_specs=[pl.BlockSpec((tm, d), lambda i, j: (i, 0)),
                  pl.BlockSpec((None, None, 1, d), lambda i, j: (i // tpb, 1, 0, 0)),
                  pl.BlockSpec((None, None, 1, d), lambda i, j: (i // tpb, 0, 0, 0)),
                  pl.BlockSpec((d, IN_TN), lambda i, j: (0, j)),
                  pl.BlockSpec((1, IN_TN), lambda i, j: (0, j)),
                  pl.BlockSpec((1, Q_LORA), lambda i, j: (0, 0)),
                  pl.BlockSpec((1, KV_LORA), lambda i, j: (0, 0))],
        out_specs=pl.BlockSpec((tm, IN_TN), lambda i, j: (i, j)),
        out_shape=jax.ShapeDtypeStruct((t, Z_WIDTH), BF16),
        scratch_shapes=[pltpu.VMEM((tm, d), BF16)],
        compiler_params=_params("arbitrary", "arbitrary"),
        name="in_proj",
    )(x2, mod4, mod4, w2, b2, gq, gkv)


def _mla_up_kernel(cq_ref, ckv_ref, kra_ref, krb_ref, cos_ref, sin_ref,
                   wq1_ref, wq2_ref, wk_ref, wv_ref, q_ref, k_ref, v_ref):
    cq = cq_ref[...]
    ckv = ckv_ref[...]
    cos = cos_ref[...]
    sin = sin_ref[...]
    qa = jnp.dot(cq, wq1_ref[...], preferred_element_type=F32)
    qb = jnp.dot(cq, wq2_ref[...], preferred_element_type=F32)
    kn = jnp.dot(ckv, wk_ref[...], preferred_element_type=F32)
    v_ref[...] = jnp.dot(ckv, wv_ref[...], preferred_element_type=F32).astype(BF16)
    k_rope = (kra_ref[...].astype(F32) * cos + krb_ref[...].astype(F32) * sin).astype(BF16)
    for h in range(B_HEADS):
        c0 = h * B_HEAD_PAD
        n0 = h * QK_NOPE
        q_ref[:, c0:c0 + QK_NOPE] = (qa[:, c0:c0 + QK_NOPE] * B_SCALE).astype(BF16)
        q_rope = qa[:, c0 + QK_NOPE:c0 + B_HEAD_PAD] * cos + qb[:, n0:n0 + QK_NOPE] * sin
        q_ref[:, c0 + QK_NOPE:c0 + B_HEAD_PAD] = (q_rope * B_SCALE).astype(BF16)
        k_ref[:, c0:c0 + QK_NOPE] = kn[:, n0:n0 + QK_NOPE].astype(BF16)
        k_ref[:, c0 + QK_NOPE:c0 + B_HEAD_PAD] = k_rope


def _mla_up(z, cos_t, sin_t, wq1, wq2, wk, wv):
    t = z.shape[0]
    tm = 512
    const = lambda i: (0, 0)
    return pl.pallas_call(
        _mla_up_kernel,
        grid=(t // tm,),
        in_specs=[pl.BlockSpec((tm, Q_LORA), lambda i: (i, CQ0 // Q_LORA)),
                  pl.BlockSpec((tm, KV_LORA), lambda i: (i, CKV0 // KV_LORA)),
                  pl.BlockSpec((tm, LANES), lambda i: (i, KR0 // LANES)),
                  pl.BlockSpec((tm, LANES), lambda i: (i, KR0 // LANES + 1)),
                  pl.BlockSpec((tm, LANES), lambda i: (i, 0)),
                  pl.BlockSpec((tm, LANES), lambda i: (i, 0)),
                  pl.BlockSpec(wq1.shape, const),
                  pl.BlockSpec(wq2.shape, const),
                  pl.BlockSpec(wk.shape, const),
                  pl.BlockSpec(wv.shape, const)],
        out_specs=[pl.BlockSpec((tm, B_HEADS * B_HEAD_PAD), lambda i: (i, 0)),
                   pl.BlockSpec((tm, B_HEADS * B_HEAD_PAD), lambda i: (i, 0)),
                   pl.BlockSpec((tm, B_WIDTH), lambda i: (i, 0))],
        out_shape=[jax.ShapeDtypeStruct((t, B_HEADS * B_HEAD_PAD), BF16),
                   jax.ShapeDtypeStruct((t, B_HEADS * B_HEAD_PAD), BF16),
                   jax.ShapeDtypeStruct((t, B_WIDTH), BF16)],
        compiler_params=_params("arbitrary"),
        name="mla_up",
    )(z, z, z, z, cos_t, sin_t, wq1, wq2, wk, wv)


def _flash_kernel(qi_ref, kj_ref, q_ref, k_ref, v_ref, o_ref, m_scr, l_scr, acc_scr):
    p = pl.program_id(2)
    i = qi_ref[p]
    j = kj_ref[p]
    tq = q_ref.shape[0]
    tk = k_ref.shape[0]

    @pl.when(j == 0)
    def _():
        m_scr[...] = jnp.full(m_scr.shape, -jnp.inf, F32)
        l_scr[...] = jnp.zeros(l_scr.shape, F32)
        acc_scr[...] = jnp.zeros(acc_scr.shape, F32)

    def update(s):
        m_prev = m_scr[...]
        m_new = jnp.maximum(m_prev, jnp.max(s, axis=1, keepdims=True))
        alpha = jnp.exp(m_prev - m_new)
        pexp = jnp.exp(s - m_new)
        l_scr[...] = alpha * l_scr[...] + jnp.sum(pexp, axis=1, keepdims=True)
        acc_scr[...] = alpha * acc_scr[...] + jnp.dot(pexp.astype(BF16), v_ref[...],
                                                      preferred_element_type=F32)
        m_scr[...] = m_new

    @pl.when(j < i)
    def _():
        update(_nt_dot(q_ref[...], k_ref[...]))

    @pl.when(j == i)
    def _():
        s = _nt_dot(q_ref[...], k_ref[...])
        qpos = lax.broadcasted_iota(jnp.int32, (tq, tk), 0)
        kpos = lax.broadcasted_iota(jnp.int32, (tq, tk), 1)
        update(jnp.where(kpos <= (qpos | (CHUNK - 1)), s, NEG))
        o_ref[...] = (acc_scr[...] / l_scr[...]).astype(BF16)


def _mla_flash(q, k, v, bsz, seq):
    tq = min(1024, seq)
    nqb = seq // tq
    pairs = [(i, j) for i in range(nqb) for j in range(i + 1)]
    qi = jnp.asarray([p[0] for p in pairs], jnp.int32)
    kj = jnp.asarray([p[1] for p in pairs], jnp.int32)
    grid_spec = pltpu.PrefetchScalarGridSpec(
        num_scalar_prefetch=2,
        grid=(bsz, B_HEADS, len(pairs)),
        in_specs=[pl.BlockSpec((tq, B_HEAD_PAD), lambda b, h, p, qi, kj: (b * nqb + qi[p], h)),
                  pl.BlockSpec((tq, B_HEAD_PAD), lambda b, h, p, qi, kj: (b * nqb + kj[p], h)),
                  pl.BlockSpec((tq, V_HEAD), lambda b, h, p, qi, kj: (b * nqb + kj[p], h))],
        out_specs=pl.BlockSpec((tq, V_HEAD), lambda b, h, p, qi, kj: (b * nqb + qi[p], h)),
        scratch_shapes=[pltpu.VMEM((tq, 1), F32), pltpu.VMEM((tq, 1), F32),
                        pltpu.VMEM((tq, V_HEAD), F32)],
    )
    return pl.pallas_call(
        _flash_kernel,
        grid_spec=grid_spec,
        out_shape=jax.ShapeDtypeStruct((bsz * seq, B_WIDTH), BF16),
        compiler_params=_params("arbitrary", "arbitrary", "arbitrary"),
        name="mla_flash",
    )(qi, kj, q, k, v)


BAND_TQ = A_LEFT_CHUNKS * CHUNK


def _band_kernel(q_ref, kp_ref, kc_ref, vp_ref, vc_ref, bias_ref, o_ref):
    i = pl.program_id(2)
    q = q_ref[...]
    first = jnp.where(i == 0, NEG, 0.0).astype(F32)
    s_prev = _nt_dot(q, kp_ref[...]) + bias_ref[:, :BAND_TQ] + first
    s_cur = _nt_dot(q, kc_ref[...]) + bias_ref[:, BAND_TQ:]
    m = jnp.maximum(jnp.max(s_prev, axis=1, keepdims=True), jnp.max(s_cur, axis=1, keepdims=True))
    p_prev = jnp.exp(s_prev - m)
    p_cur = jnp.exp(s_cur - m)
    l = jnp.sum(p_prev, axis=1, keepdims=True) + jnp.sum(p_cur, axis=1, keepdims=True)
    o = (jnp.dot(p_prev.astype(BF16), vp_ref[...], preferred_element_type=F32)
         + jnp.dot(p_cur.astype(BF16), vc_ref[...], preferred_element_type=F32))
    o_ref[...] = (o / l).astype(BF16)


def _band_bias(rel_bias):
    period = 3 * BAND_TQ
    d = jnp.arange(period)
    d = jnp.where(d > 2 * BAND_TQ, d - period, d)
    ridx = jnp.clip(BAND_TQ - d, -REL_CLIP_RIGHT, REL_CLIP_LEFT) + REL_CLIP_RIGHT
    base = rel_bias[:, ridx].astype(F32)
    heads = rel_bias.shape[0]
    toep = jnp.tile(base, (1, BAND_TQ))[:, :BAND_TQ * (period - 1)]
    toep = toep.reshape(heads, BAND_TQ, period - 1)[:, :, :2 * BAND_TQ]
    qc = (jnp.arange(BAND_TQ)[:, None] + BAND_TQ) // CHUNK
    kc = jnp.arange(2 * BAND_TQ)[None, :] // CHUNK
    valid = jnp.logical_and(kc <= qc, kc >= qc - A_LEFT_CHUNKS)
    return jnp.where(valid[None], toep, NEG)


def _band_attn(z, bias, bsz, seq):
    nq = seq // BAND_TQ
    qb, kb, vb = QA0 // A_HEAD_DIM, KA0 // A_HEAD_DIM, VA0 // A_HEAD_DIM
    cur = lambda c0: (lambda h, b, i: (b * nq + i, c0 + h))
    prev = lambda c0: (lambda h, b, i: (b * nq + jnp.maximum(i - 1, 0), c0 + h))
    blk = (BAND_TQ, A_HEAD_DIM)
    return pl.pallas_call(
        _band_kernel,
        grid=(A_HEADS, bsz, nq),
        in_specs=[pl.BlockSpec(blk, cur(qb)),
                  pl.BlockSpec(blk, prev(kb)), pl.BlockSpec(blk, cur(kb)),
                  pl.BlockSpec(blk, prev(vb)), pl.BlockSpec(blk, cur(vb)),
                  pl.BlockSpec((None, BAND_TQ, 2 * BAND_TQ), lambda h, b, i: (h, 0, 0))],
        out_specs=pl.BlockSpec(blk, lambda h, b, i: (b * nq + i, h)),
        out_shape=jax.ShapeDtypeStruct((bsz * seq, A_WIDTH), BF16),
        compiler_params=_params("arbitrary", "arbitrary", "arbitrary"),
        name="band_attn",
    )(z, z, z, z, z, bias)


def _outproj_kernel(ya_ref, yb_ref, ga_ref, gb_ref, x_ref, g1_ref, sc2_ref, sh2_ref,
                    wpa_ref, wpb_ref, wo_ref, lg_ref, lb_ref, x1_ref, h2_ref):
    ta = jnp.dot(ya_ref[...], wpa_ref[...], preferred_element_type=F32)
    tb = jnp.dot(yb_ref[...], wpb_ref[...], preferred_element_type=F32)
    u = ga_ref[...].astype(F32) * ta + gb_ref[...].astype(F32) * tb
    y = jnp.dot(u.astype(BF16), wo_ref[...], preferred_element_type=F32)
    x1 = _layer_norm(DN_ALPHA * x_ref[...] + g1_ref[...] * y, lg_ref[...], lb_ref[...])
    x1_ref[...] = x1
    h2_ref[...] = (x1 * (1.0 + sc2_ref[...]) + sh2_ref[...]).astype(BF16)


def _out_proj(ya, yb, z, x2, mod4, wpa, wpb, wo, lg, lb, seq):
    t, d = x2.shape
    tm = 256
    tpb = seq // tm
    const = lambda i: (0, 0)
    modspec = lambda k: pl.BlockSpec((None, None, 1, d), lambda i: (i // tpb, k, 0, 0))
    return pl.pallas_call(
        _outproj_kernel,
        grid=(t // tm,),
        in_specs=[pl.BlockSpec((tm, A_WIDTH), lambda i: (i, 0)),
                  pl.BlockSpec((tm, B_WIDTH), lambda i: (i, 0)),
                  pl.BlockSpec((tm, d), lambda i: (i, GA0 // d)),
                  pl.BlockSpec((tm, d), lambda i: (i, GB0 // d)),
                  pl.BlockSpec((tm, d), lambda i: (i, 0)),
                  modspec(2), modspec(4), modspec(3),
                  pl.BlockSpec(wpa.shape, const), pl.BlockSpec(wpb.shape, const),
                  pl.BlockSpec(wo.shape, const),
                  pl.BlockSpec((1, d), const), pl.BlockSpec((1, d), const)],
        out_specs=[pl.BlockSpec((tm, d), lambda i: (i, 0)),
                   pl.BlockSpec((tm, d), lambda i: (i, 0))],
        out_shape=[jax.ShapeDtypeStruct((t, d), F32), jax.ShapeDtypeStruct((t, d), BF16)],
        compiler_params=_params("arbitrary"),
        name="out_proj",
    )(ya, yb, z, z, x2, mod4, mod4, mod4, wpa, wpb, wo, lg, lb)


TOPV = PEER_TOPK + 1
TOPV_ROWS = 24
COMB_DENSE = 8
COMB_ROWS = COMB_DENSE * PEER_TOPK + 2 * (TOPV_ROWS - COMB_DENSE)


def _top_vals(s, out_scr):
    for r in range(TOPV):
        m = jnp.max(s, axis=0, keepdims=True)
        out_scr[r:r + 1, :] = m
        if r + 1 < TOPV:
            s = jnp.where(s == m, -jnp.inf, s)


def _topk_kernel(h2_ref, wqt_ref, keys_ref, tau_ref, e1_ref, s2_ref, e2_ref,
                 st_scr, tv1_scr, tv2_scr, comb_scr, fv_scr):
    pqt = _nt_dot(wqt_ref[...], h2_ref[...]).astype(BF16)
    for hp in range(2 * PEER_HEADS):
        r0 = hp * PEER_HALF
        st_scr[r0:r0 + N_KEYS, :] = jnp.dot(keys_ref[hp], pqt[r0:r0 + PEER_HALF, :],
                                            preferred_element_type=F32)
    pad = jnp.full((TOPV_ROWS - PEER_TOPK, h2_ref.shape[0]), -jnp.inf, F32)
    tv1_scr[PEER_TOPK:, :] = pad
    tv2_scr[PEER_TOPK:, :] = pad

    def head(h, carry):
        r1 = pl.multiple_of(h * 2 * N_KEYS, 2 * N_KEYS)
        s1 = st_scr[pl.ds(r1, N_KEYS), :]
        s2 = st_scr[pl.ds(r1 + N_KEYS, N_KEYS), :]
        _top_vals(s1, tv1_scr)
        _top_vals(s2, tv2_scr)
        tv2 = tv2_scr[0:PEER_TOPK, :]
        for i in range(COMB_DENSE):
            comb_scr[i * PEER_TOPK:(i + 1) * PEER_TOPK, :] = tv1_scr[i:i + 1, :] + tv2
        c0 = COMB_DENSE * PEER_TOPK
        c1 = c0 + TOPV_ROWS - COMB_DENSE
        comb_scr[c0:c1, :] = tv1_scr[COMB_DENSE:, :] + tv2_scr[0:1, :]
        comb_scr[c1:, :] = tv2_scr[COMB_DENSE:, :] + tv1_scr[0:1, :]
        _top_vals(comb_scr[...], fv_scr)
        top = fv_scr[0:1, :]
        kept = fv_scr[0:PEER_TOPK, :]
        theta = 0.5 * (fv_scr[PEER_TOPK - 1:PEER_TOPK, :] + fv_scr[PEER_TOPK:PEER_TOPK + 1, :])
        inv_z = 1.0 / jnp.sum(jnp.exp(kept - top), axis=0, keepdims=True)
        tau_ref[h] = theta - s1
        e1_ref[h] = jnp.exp(s1 - tv1_scr[0:1, :]) * inv_z
        s2_ref[h] = s2
        e2_ref[h] = jnp.exp(s2 - tv2_scr[0:1, :])
        return carry

    lax.fori_loop(0, PEER_HEADS, head, 0)


def _peer_topk(h2, wqt, keys_b):
    t, d = h2.shape
    tm = 512
    rows = PEER_HEADS * 2 * N_KEYS
    out = jax.ShapeDtypeStruct((PEER_HEADS, N_KEYS, t), F32)
    ospec = pl.BlockSpec((PEER_HEADS, N_KEYS, tm), lambda i: (0, 0, i))
    return pl.pallas_call(
        _topk_kernel,
        grid=(t // tm,),
        in_specs=[pl.BlockSpec((tm, d), lambda i: (i, 0)),
                  pl.BlockSpec(wqt.shape, lambda i: (0, 0)),
                  pl.BlockSpec(keys_b.shape, lambda i: (0, 0, 0))],
        out_specs=[ospec] * 4,
        out_shape=[out] * 4,
        scratch_shapes=[pltpu.VMEM((rows, tm), F32),
                        pltpu.VMEM((TOPV_ROWS, tm), F32),
                        pltpu.VMEM((TOPV_ROWS, tm), F32),
                        pltpu.VMEM((COMB_ROWS, tm), F32),
                        pltpu.VMEM((TOPV_ROWS, tm), F32)],
        compiler_params=_params("arbitrary"),
        name="peer_topk",
    )(h2, wqt, keys_b)


PEER_TE = 1024
PEER_TM = 512
GATE_K1_TILE = 2


def _dense_stages(u_blk, vt_blk, h2_s, tau_s, e1_s, s2_s, e2_s, acc_s, at_new, at_old, wt_new, wt_old):
    acc_s[...] += jnp.dot(vt_blk[...], wt_old[...], preferred_element_type=F32)
    at_new[...] = _nt_dot(u_blk[...], h2_s[...])
    rows = SUBLANES_BF16
    for lb in range(PEER_TM // LANES):
        lanes = slice(lb * LANES, (lb + 1) * LANES)
        for g in range(N_KEYS // rows):
            krows = slice(g * rows, (g + 1) * rows)
            for k0 in range(0, PEER_TE // N_KEYS, GATE_K1_TILE):
                gates = [jnp.zeros((rows, LANES), F32) for _ in range(GATE_K1_TILE)]
                for h in range(PEER_HEADS):
                    s2 = s2_s[h, krows, lanes]
                    e2 = e2_s[h, krows, lanes]
                    for j in range(GATE_K1_TILE):
                        tau = tau_s[h, k0 + j:k0 + j + 1, lanes]
                        e1 = e1_s[h, k0 + j:k0 + j + 1, lanes]
                        gates[j] = gates[j] + jnp.where(s2 >= tau, e2, 0.0) * e1
                for j in range(GATE_K1_TILE):
                    r0 = (k0 + j) * N_KEYS + g * rows
                    a = at_old[r0:r0 + rows, lanes]
                    w = 0.5 * a * (1.0 + lax.erf(a * SQRT_HALF)) * gates[j]
                    wt_new[r0:r0 + rows, lanes] = w.astype(BF16)


def _dense_kernel(h2_ref, u_hbm, vt_hbm, tau_ref, e1_ref, s2_ref, e2_ref, pt_ref,
                  u_buf, vt_buf, u_sem, vt_sem, at0_scr, at1_scr, wt0_scr, wt1_scr,
                  acc_s, h2_s, tau_s, e1_s, s2_s, e2_s, *, n_e, n_blocks):
    s = pl.program_id(0)
    n_steps = n_blocks + 2
    blk = lambda step, lag: jnp.clip(step - lag, 0, n_blocks - 1)
    e_a = lax.rem(blk(s, 0), n_e)
    e_b = lax.rem(blk(s, 1), n_e)
    e_c = lax.rem(blk(s, 2), n_e)

    def u_copy(step, slot):
        rows = pl.ds(pl.multiple_of(lax.rem(blk(step, 0), n_e) * PEER_TE, PEER_TE), PEER_TE)
        return pltpu.make_async_copy(u_hbm.at[rows, :], u_buf.at[slot], u_sem.at[slot])

    def vt_copy(step, slot):
        cols = pl.ds(pl.multiple_of(lax.rem(blk(step, 2), n_e) * PEER_TE, PEER_TE), PEER_TE)
        return pltpu.make_async_copy(vt_hbm.at[:, cols], vt_buf.at[slot], vt_sem.at[slot])

    @pl.when(s == 0)
    def _():
        for scr in (at0_scr, at1_scr, wt0_scr, wt1_scr):
            scr[...] = jnp.zeros(scr.shape, scr.dtype)
        u_copy(s, 0).start()
        vt_copy(s, 0).start()

    @pl.when(e_a == 0)
    def _():
        h2_s[...] = h2_ref[...]

    @pl.when(e_b == 0)
    def _():
        s2_s[...] = s2_ref[...]
        e2_s[...] = e2_ref[...]

    @pl.when(e_c == 0)
    def _():
        acc_s[...] = jnp.zeros(acc_s.shape, F32)

    def step_body(slot, at_new, at_old, wt_new, wt_old):
        @pl.when(s + 1 < n_steps)
        def _():
            u_copy(s + 1, 1 - slot).start()
            vt_copy(s + 1, 1 - slot).start()

        u_copy(s, slot).wait()
        vt_copy(s, slot).wait()
        tau_s[...] = tau_ref[...]
        e1_s[...] = e1_ref[...]
        _dense_stages(u_buf.at[slot], vt_buf.at[slot], h2_s, tau_s, e1_s, s2_s, e2_s, acc_s,
                      at_new, at_old, wt_new, wt_old)

    @pl.when(lax.rem(s, 2) == 0)
    def _():
        step_body(0, at0_scr, at1_scr, wt1_scr, wt0_scr)

    @pl.when(lax.rem(s, 2) == 1)
    def _():
        step_body(1, at1_scr, at0_scr, wt0_scr, wt1_scr)

    @pl.when(jnp.logical_and(e_c == n_e - 1, s >= 2))
    def _():
        pt_ref[...] = acc_s[...]


def _peer_dense(h2, u_b, vt_b, tau, e1, s2, e2):
    t, d = h2.shape
    n_e = u_b.shape[0] // PEER_TE
    n_blocks = (t // PEER_TM) * n_e
    k1_blk = PEER_TE // N_KEYS
    blk = lambda lag: (lambda s: jnp.clip(s - lag, 0, n_blocks - 1))
    tile = lambda lag: (lambda s: blk(lag)(s) // n_e)
    eblk = lambda lag: (lambda s: lax.rem(blk(lag)(s), n_e))
    rspec = pl.BlockSpec((PEER_HEADS, k1_blk, PEER_TM), lambda s: (0, eblk(1)(s), tile(1)(s)))
    fspec = pl.BlockSpec((PEER_HEADS, N_KEYS, PEER_TM), lambda s: (0, 0, tile(1)(s)),
                         pipeline_mode=pl.Buffered(1))
    any_spec = pl.BlockSpec(memory_space=pl.ANY)
    return pl.pallas_call(
        functools.partial(_dense_kernel, n_e=n_e, n_blocks=n_blocks),
        grid=(n_blocks + 2,),
        in_specs=[pl.BlockSpec((PEER_TM, d), lambda s: (tile(0)(s), 0)),
                  any_spec, any_spec, rspec, rspec, fspec, fspec],
        out_specs=pl.BlockSpec((d, PEER_TM), lambda s: (0, tile(2)(s))),
        out_shape=jax.ShapeDtypeStruct((d, t), F32),
        scratch_shapes=[pltpu.VMEM((2, PEER_TE, d), BF16), pltpu.VMEM((2, d, PEER_TE), BF16),
                        pltpu.SemaphoreType.DMA((2,)), pltpu.SemaphoreType.DMA((2,)),
                        pltpu.VMEM((PEER_TE, PEER_TM), F32), pltpu.VMEM((PEER_TE, PEER_TM), F32),
                        pltpu.VMEM((PEER_TE, PEER_TM), BF16), pltpu.VMEM((PEER_TE, PEER_TM), BF16),
                        pltpu.VMEM((d, PEER_TM), F32), pltpu.VMEM((PEER_TM, d), BF16),
                        pltpu.VMEM((PEER_HEADS, k1_blk, PEER_TM), F32),
                        pltpu.VMEM((PEER_HEADS, k1_blk, PEER_TM), F32),
                        pltpu.VMEM((PEER_HEADS, N_KEYS, PEER_TM), F32),
                        pltpu.VMEM((PEER_HEADS, N_KEYS, PEER_TM), F32)],
        compiler_params=_params("arbitrary"),
        name="peer_dense",
    )(h2, u_b, vt_b, tau, e1, s2, e2)


def _final_kernel(pt_ref, x1_ref, g2_ref, lg_ref, lb_ref, o_ref):
    p = pt_ref[...].T
    o_ref[...] = _layer_norm(DN_ALPHA * x1_ref[...] + g2_ref[...] * p, lg_ref[...], lb_ref[...])


def _peer_final(pt, x1, mod4, lg, lb, seq):
    t, d = x1.shape
    tm = 512
    tpb = seq // tm
    return pl.pallas_call(
        _final_kernel,
        grid=(t // tm,),
        in_specs=[pl.BlockSpec((d, tm), lambda i: (0, i)),
                  pl.BlockSpec((tm, d), lambda i: (i, 0)),
                  pl.BlockSpec((None, None, 1, d), lambda i: (i // tpb, 5, 0, 0)),
                  pl.BlockSpec((1, d), lambda i: (0, 0)),
                  pl.BlockSpec((1, d), lambda i: (0, 0))],
        out_specs=pl.BlockSpec((tm, d), lambda i: (i, 0)),
        out_shape=jax.ShapeDtypeStruct((t, d), F32),
        compiler_params=_params("arbitrary"),
        name="peer_final",
    )(pt, x1, mod4, lg, lb)


def _half_swap(w):
    half = w.shape[-1] // 2
    return jnp.concatenate([w[..., half:], w[..., :half]], axis=-1)


def _in_proj_layout(w, pad_to):
    o = np.cumsum([0, A_WIDTH, A_WIDTH, A_WIDTH, Q_LORA, KV_LORA, QK_ROPE, pad_to, pad_to])
    qa, ka, va, cq, ckv, kr, ga, gb = [w[..., o[n]:o[n + 1]] for n in range(8)]
    z64 = jnp.zeros(kr.shape[:-1] + (LANES - QK_ROPE,), w.dtype)
    ztail = jnp.zeros(kr.shape[:-1] + (IN_TN - 2 * LANES,), w.dtype)
    return jnp.concatenate([ga, gb, qa, ka, va, cq, ckv, kr, z64, _half_swap(kr), z64, ztail], axis=-1)


def _mla_weight_layout(w_uq, w_ukv):
    r = w_uq.shape[0]
    uq = w_uq.reshape(r, B_HEADS, QK_NOPE + QK_ROPE)
    nope, rope = uq[..., :QK_NOPE], uq[..., QK_NOPE:]
    z64 = jnp.zeros((r, B_HEADS, B_HEAD_PAD - QK_NOPE - QK_ROPE), w_uq.dtype)
    wq1 = jnp.concatenate([nope, rope, z64], axis=-1).reshape(r, B_HEADS * B_HEAD_PAD)
    wq2 = jnp.concatenate([_half_swap(rope), z64], axis=-1).reshape(r, B_HEADS * LANES)
    ukv = w_ukv.reshape(w_ukv.shape[0], B_HEADS, QK_NOPE + V_HEAD)
    wk = ukv[..., :QK_NOPE].reshape(-1, B_HEADS * QK_NOPE)
    wv = ukv[..., QK_NOPE:].reshape(-1, B_WIDTH)
    return wq1.astype(BF16), wq2.astype(BF16), wk.astype(BF16), wv.astype(BF16)


def kernel(x, c, positions, w_ada, b_ada, w_in, b_in, rel_bias, q_norm_g, kv_norm_g, w_uq, w_ukv,
           w_pa, w_pb, w_o, ln1_g, ln1_b, peer_wq, peer_keys, peer_u, peer_v, ln2_g, ln2_b):
    bsz, seq, d = x.shape
    t = bsz * seq
    assert d == GB0 - GA0 and seq % 1024 == 0 and w_ada.shape[0] == 1
    x2 = x.reshape(t, d)
    row = lambda a: a.reshape(1, -1).astype(F32)

    mod4 = _ada_mod(c, w_ada[0], b_ada[0]).reshape(bsz, 6, 1, d)
    cos_t, sin_t = _rope_tables(positions)

    w2 = _in_proj_layout(w_in[0], d).astype(BF16)
    b2 = _in_proj_layout(b_in[0], d).reshape(1, Z_WIDTH).astype(F32)
    z = _in_proj(x2, mod4, w2, b2, row(q_norm_g[0]), row(kv_norm_g[0]), seq)

    wq1, wq2, wk, wv = _mla_weight_layout(w_uq[0], w_ukv[0])
    qb, kb, vb = _mla_up(z, cos_t, sin_t, wq1, wq2, wk, wv)
    yb = _mla_flash(qb, kb, vb, bsz, seq)
    ya = _band_attn(z, _band_bias(rel_bias[0]), bsz, seq)

    x1, h2 = _out_proj(ya, yb, z, x2, mod4, w_pa[0].astype(BF16), w_pb[0].astype(BF16),
                       w_o[0].astype(BF16), row(ln1_g[0]), row(ln1_b[0]), seq)

    wqt = peer_wq[0].astype(BF16).T
    keys_b = peer_keys[0].reshape(2 * PEER_HEADS, N_KEYS, PEER_HALF).astype(BF16)
    tau, e1, s2, e2 = _peer_topk(h2, wqt, keys_b)
    pt = _peer_dense(h2, peer_u[0].astype(BF16), peer_v[0].astype(BF16).T, tau, e1, s2, e2)
    out = _peer_final(pt, x1, mod4, row(ln2_g[0]), row(ln2_b[0]), seq)
    return out.reshape(bsz, seq, d)
```

```python
import functools
import math

import jax
import jax.numpy as jnp
import numpy as np
from jax import lax
from jax.experimental import pallas as pl
from jax.experimental.pallas import tpu as pltpu

F32 = jnp.float32
BF16 = jnp.bfloat16

CHUNK = 64
A_HEADS = 8
A_HEAD_DIM = 128
A_LEFT_CHUNKS = 8
REL_CLIP_LEFT = 256
REL_CLIP_RIGHT = CHUNK - 1
B_HEADS = 8
Q_LORA = 512
KV_LORA = 512
QK_NOPE = 128
QK_ROPE = 64
V_HEAD = 128
ROPE_THETA = 10000.0
N_KEYS = 128
PEER_HEADS = 8
PEER_TOPK = 16
PEER_HALF = 128
DEPTH = 1
DN_ALPHA = (2 * DEPTH) ** 0.25
LN_EPS = 1e-5
RMS_EPS = 1e-6
NEG = -1e30

A_WIDTH = A_HEADS * A_HEAD_DIM
B_WIDTH = B_HEADS * V_HEAD
LOG2E = math.log2(math.e)
A_QSCALE = A_HEAD_DIM ** -0.5 * LOG2E
B_QSCALE = (QK_NOPE + QK_ROPE) ** -0.5 * LOG2E
SQRT_HALF = math.sqrt(0.5)

LANES = 128
SUBLANES_BF16 = 16
VMEM_LIMIT_BYTES = 56 * 1024 * 1024

IN_TN = 512
GA0 = 0
GB0 = 2048
QA0 = 4096
KA0 = 5120
VA0 = 6144
CQ0 = 7168
CKV0 = 7680
KR0 = 8192
Z_WIDTH = 8704
B_HEAD_PAD = 256


def _nt_dot(a, b):
    return lax.dot_general(a, b, (((1,), (1,)), ((), ())), preferred_element_type=F32)


def _params(*sem, flags=None, vmem=VMEM_LIMIT_BYTES):
    return pltpu.CompilerParams(dimension_semantics=sem, vmem_limit_bytes=vmem, flags=flags)


def _layer_norm(r, g, b):
    mu = jnp.mean(r, axis=-1, keepdims=True)
    d = r - mu
    var = jnp.mean(d * d, axis=-1, keepdims=True)
    return d * lax.rsqrt(var + LN_EPS) * g + b


def _ada_kernel(c_ref, w_ref, b_ref, o_ref):
    c = c_ref[...]
    cond = (c * jax.nn.sigmoid(c)).astype(BF16)
    o_ref[...] = jnp.dot(cond, w_ref[...].astype(BF16), preferred_element_type=F32) + b_ref[...]


def _ada_mod(c, w_ada, b_ada):
    bsz, d = c.shape
    n = w_ada.shape[1]
    tn = 1024
    return pl.pallas_call(
        _ada_kernel,
        grid=(n // tn,),
        in_specs=[pl.BlockSpec((bsz, d), lambda j: (0, 0)),
                  pl.BlockSpec((d, tn), lambda j: (0, j)),
                  pl.BlockSpec((1, tn), lambda j: (0, j))],
        out_specs=pl.BlockSpec((bsz, tn), lambda j: (0, j)),
        out_shape=jax.ShapeDtypeStruct((bsz, n), F32),
        compiler_params=_params("arbitrary"),
        name="ada_mod",
    )(c, w_ada, b_ada.reshape(1, n))


def _rope_kernel(pos_ref, invf_ref, cos_ref, sin_ref):
    ang = pos_ref[...] * invf_ref[...]
    cos_ref[...] = jnp.cos(ang)
    sin_ref[...] = jnp.sin(ang)


def _rope_tables(positions):
    t = positions.size
    half = QK_ROPE // 2
    per_row = LANES // half
    rows = t // per_row
    inv_freq = ROPE_THETA ** (-jnp.arange(0, QK_ROPE, 2, dtype=F32) / QK_ROPE)
    pos_rep = jnp.repeat(positions.astype(F32).reshape(rows, per_row), half, axis=1)
    invf = jnp.tile(inv_freq, per_row).reshape(1, LANES)
    tr = min(rows, 2048)
    cos, sin = pl.pallas_call(
        _rope_kernel,
        grid=(rows // tr,),
        in_specs=[pl.BlockSpec((tr, LANES), lambda i: (i, 0)),
                  pl.BlockSpec((1, LANES), lambda i: (0, 0))],
        out_specs=[pl.BlockSpec((tr, LANES), lambda i: (i, 0))] * 2,
        out_shape=[jax.ShapeDtypeStruct((rows, LANES), F32)] * 2,
        compiler_params=_params("arbitrary"),
        name="rope_tab",
    )(pos_rep, invf)
    cos = cos.reshape(t, half)
    sin = sin.reshape(t, half)
    zeros = jnp.zeros((t, LANES - QK_ROPE), F32)
    return (jnp.concatenate([cos, cos, zeros], axis=1),
            jnp.concatenate([-sin, sin, zeros], axis=1))


def _inproj_kernel(x_ref, sc_ref, sh_ref, w_ref, b_ref, gq_ref, gkv_ref, z_ref, h_scr):
    j = pl.program_id(1)

    @pl.when(j == 0)
    def _():
        h_scr[...] = (x_ref[...] * (1.0 + sc_ref[...]) + sh_ref[...]).astype(BF16)

    acc = jnp.dot(h_scr[...], w_ref[...], preferred_element_type=F32) + b_ref[...]

    def rms(g):
        ms = jnp.mean(acc * acc, axis=-1, keepdims=True)
        return acc * lax.rsqrt(ms + RMS_EPS) * g

    @pl.when(j < QA0 // IN_TN)
    def _():
        z_ref[...] = jax.nn.sigmoid(acc).astype(BF16)

    @pl.when(jnp.logical_and(j >= QA0 // IN_TN, j < KA0 // IN_TN))
    def _():
        z_ref[...] = (acc * A_QSCALE).astype(BF16)

    @pl.when(jnp.logical_or(jnp.logical_and(j >= KA0 // IN_TN, j < CQ0 // IN_TN), j == KR0 // IN_TN))
    def _():
        z_ref[...] = acc.astype(BF16)

    @pl.when(j == CQ0 // IN_TN)
    def _():
        z_ref[...] = rms(gq_ref[...]).astype(BF16)

    @pl.when(j == CKV0 // IN_TN)
    def _():
        z_ref[...] = rms(gkv_ref[...]).astype(BF16)


def _in_proj(x2, mod4, w2, b2, gq, gkv, seq):
    t, d = x2.shape
    tm = min(1024, seq)
    tpb = seq // tm
    return pl.pallas_call(
        _inproj_kernel,
        grid=(t // tm, Z_WIDTH // IN_TN),
        in_specs=[pl.BlockSpec((tm, d), lambda i, j: (i, 0)),
                  pl.BlockSpec((None, None, 1, d), lambda i, j: (i // tpb, 1, 0, 0)),
                  pl.BlockSpec((None, None, 1, d), lambda i, j: (i // tpb, 0, 0, 0)),
                  pl.BlockSpec((d, IN_TN), lambda i, j: (0, j)),
                  pl.BlockSpec((1, IN_TN), lambda i, j: (0, j)),
                  pl.BlockSpec((1, Q_LORA), lambda i, j: (0, 0)),
                  pl.BlockSpec((1, KV_LORA), lambda i, j: (0, 0))],
        out_specs=pl.BlockSpec((tm, IN_TN), lambda i, j: (i, j)),
        out_shape=jax.ShapeDtypeStruct((t, Z_WIDTH), BF16),
        scratch_shapes=[pltpu.VMEM((tm, d), BF16)],
        compiler_params=_params("arbitrary", "arbitrary"),
        name="in_proj",
    )(x2, mod4, mod4, w2, b2, gq, gkv)


def _mla_up_kernel(cq_ref, ckv_ref, kra_ref, krb_ref, cos_ref, sin_ref,
                   wq1_ref, wq2_ref, wk_ref, wv_ref, q_ref, k_ref, v_ref):
    cq = cq_ref[...]
    ckv = ckv_ref[...]
    cos = cos_ref[...]
    sin = sin_ref[...]
    qa = jnp.dot(cq, wq1_ref[...], preferred_element_type=F32)
    qb = jnp.dot(cq, wq2_ref[...], preferred_element_type=F32)
    kn = jnp.dot(ckv, wk_ref[...], preferred_element_type=F32)
    v_ref[...] = jnp.dot(ckv, wv_ref[...], preferred_element_type=F32).astype(BF16)
    k_rope = (kra_ref[...].astype(F32) * cos + krb_ref[...].astype(F32) * sin).astype(BF16)
    for h in range(B_HEADS):
        c0 = h * B_HEAD_PAD
        n0 = h * QK_NOPE
        q_ref[:, c0:c0 + QK_NOPE] = (qa[:, c0:c0 + QK_NOPE] * B_QSCALE).astype(BF16)
        q_rope = qa[:, c0 + QK_NOPE:c0 + B_HEAD_PAD] * cos + qb[:, n0:n0 + QK_NOPE] * sin
        q_ref[:, c0 + QK_NOPE:c0 + B_HEAD_PAD] = (q_rope * B_QSCALE).astype(BF16)
        k_ref[:, c0:c0 + QK_NOPE] = kn[:, n0:n0 + QK_NOPE].astype(BF16)
        k_ref[:, c0 + QK_NOPE:c0 + B_HEAD_PAD] = k_rope


def _mla_up(z, cos_t, sin_t, wq1, wq2, wk, wv):
    t = z.shape[0]
    tm = 512
    const = lambda i: (0, 0)
    return pl.pallas_call(
        _mla_up_kernel,
        grid=(t // tm,),
        in_specs=[pl.BlockSpec((tm, Q_LORA), lambda i: (i, CQ0 // Q_LORA)),
                  pl.BlockSpec((tm, KV_LORA), lambda i: (i, CKV0 // KV_LORA)),
                  pl.BlockSpec((tm, LANES), lambda i: (i, KR0 // LANES)),
                  pl.BlockSpec((tm, LANES), lambda i: (i, KR0 // LANES + 1)),
                  pl.BlockSpec((tm, LANES), lambda i: (i, 0)),
                  pl.BlockSpec((tm, LANES), lambda i: (i, 0)),
                  pl.BlockSpec(wq1.shape, const),
                  pl.BlockSpec(wq2.shape, const),
                  pl.BlockSpec(wk.shape, const),
                  pl.BlockSpec(wv.shape, const)],
        out_specs=[pl.BlockSpec((tm, B_HEADS * B_HEAD_PAD), lambda i: (i, 0)),
                   pl.BlockSpec((tm, B_HEADS * B_HEAD_PAD), lambda i: (i, 0)),
                   pl.BlockSpec((tm, B_WIDTH), lambda i: (i, 0))],
        out_shape=[jax.ShapeDtypeStruct((t, B_HEADS * B_HEAD_PAD), BF16),
                   jax.ShapeDtypeStruct((t, B_HEADS * B_HEAD_PAD), BF16),
                   jax.ShapeDtypeStruct((t, B_WIDTH), BF16)],
        compiler_params=_params("arbitrary"),
        name="mla_up",
    )(z, z, z, z, cos_t, sin_t, wq1, wq2, wk, wv)


FLASH_HEADS = 2


def _flash_kernel(qi_ref, kj_ref, q_ref, k_ref, v_ref, o_ref, q_s, k_s, v_s, m_scr, l_scr, acc_scr):
    p = pl.program_id(2)
    i = qi_ref[p]
    j = kj_ref[p]
    tq = q_ref.shape[0]
    tk = k_ref.shape[0]

    @pl.when(j == 0)
    def _():
        m_scr[...] = jnp.full(m_scr.shape, -jnp.inf, F32)
        l_scr[...] = jnp.zeros(l_scr.shape, F32)
        acc_scr[...] = jnp.zeros(acc_scr.shape, F32)

    def step(diagonal):
        q_s[...] = q_ref[...]
        k_s[...] = k_ref[...]
        v_s[...] = v_ref[...]
        for hh in range(FLASH_HEADS):
            qk = slice(hh * B_HEAD_PAD, (hh + 1) * B_HEAD_PAD)
            vv = slice(hh * V_HEAD, (hh + 1) * V_HEAD)
            s = _nt_dot(q_s[:, qk], k_s[:, qk])
            if diagonal:
                qpos = lax.broadcasted_iota(jnp.int32, (tq, tk), 0)
                kpos = lax.broadcasted_iota(jnp.int32, (tq, tk), 1)
                s = jnp.where(kpos <= (qpos | (CHUNK - 1)), s, NEG)
            m_prev = m_scr[hh]
            m_new = jnp.maximum(m_prev, jnp.max(s, axis=1, keepdims=True))
            alpha = jnp.exp2(m_prev - m_new)
            pexp = jnp.exp2(s - m_new)
            l_new = alpha * l_scr[hh] + jnp.sum(pexp, axis=1, keepdims=True)
            acc = alpha * acc_scr[hh] + jnp.dot(pexp.astype(BF16), v_s[:, vv],
                                                preferred_element_type=F32)
            m_scr[hh] = m_new
            l_scr[hh] = l_new
            acc_scr[hh] = acc
            if diagonal:
                o_ref[:, vv] = (acc / l_new).astype(BF16)

    @pl.when(j < i)
    def _():
        step(False)

    @pl.when(j == i)
    def _():
        step(True)


def _mla_flash(q, k, v, bsz, seq):
    tq = min(1024, seq)
    nqb = seq // tq
    pairs = [(i, j) for i in range(nqb) for j in range(i + 1)]
    qi = jnp.asarray([p[0] for p in pairs], jnp.int32)
    kj = jnp.asarray([p[1] for p in pairs], jnp.int32)
    qk_w = FLASH_HEADS * B_HEAD_PAD
    v_w = FLASH_HEADS * V_HEAD
    grid_spec = pltpu.PrefetchScalarGridSpec(
        num_scalar_prefetch=2,
        grid=(bsz, B_HEADS // FLASH_HEADS, len(pairs)),
        in_specs=[pl.BlockSpec((tq, qk_w), lambda b, h, p, qi, kj: (b * nqb + qi[p], h)),
                  pl.BlockSpec((tq, qk_w), lambda b, h, p, qi, kj: (b * nqb + kj[p], h)),
                  pl.BlockSpec((tq, v_w), lambda b, h, p, qi, kj: (b * nqb + kj[p], h))],
        out_specs=pl.BlockSpec((tq, v_w), lambda b, h, p, qi, kj: (b * nqb + qi[p], h)),
        scratch_shapes=[pltpu.VMEM((tq, qk_w), BF16), pltpu.VMEM((tq, qk_w), BF16),
                        pltpu.VMEM((tq, v_w), BF16),
                        pltpu.VMEM((FLASH_HEADS, tq, 1), F32), pltpu.VMEM((FLASH_HEADS, tq, 1), F32),
                        pltpu.VMEM((FLASH_HEADS, tq, V_HEAD), F32)],
    )
    return pl.pallas_call(
        _flash_kernel,
        grid_spec=grid_spec,
        out_shape=jax.ShapeDtypeStruct((bsz * seq, B_WIDTH), BF16),
        compiler_params=_params("arbitrary", "arbitrary", "arbitrary"),
        name="mla_flash",
    )(qi, kj, q, k, v)


BAND_TQ = A_LEFT_CHUNKS * CHUNK


BAND_HEADS = 2


def _band_kernel(q_ref, kp_ref, kc_ref, vp_ref, vc_ref, bias_ref, o_ref):
    i = pl.program_id(2)
    first = jnp.where(i == 0, NEG, 0.0).astype(F32)
    outs = []
    for hh in range(BAND_HEADS):
        cols = slice(hh * A_HEAD_DIM, (hh + 1) * A_HEAD_DIM)
        q = q_ref[:, cols]
        s_prev = _nt_dot(q, kp_ref[:, cols]) + bias_ref[hh, :, :BAND_TQ] + first
        s_cur = _nt_dot(q, kc_ref[:, cols]) + bias_ref[hh, :, BAND_TQ:]
        m = jnp.maximum(jnp.max(s_prev, axis=1, keepdims=True), jnp.max(s_cur, axis=1, keepdims=True))
        p_prev = jnp.exp2(s_prev - m)
        p_cur = jnp.exp2(s_cur - m)
        l = jnp.sum(p_prev, axis=1, keepdims=True) + jnp.sum(p_cur, axis=1, keepdims=True)
        o = (jnp.dot(p_prev.astype(BF16), vp_ref[:, cols], preferred_element_type=F32)
             + jnp.dot(p_cur.astype(BF16), vc_ref[:, cols], preferred_element_type=F32))
        outs.append((o / l).astype(BF16))
    for hh in range(BAND_HEADS):
        o_ref[:, hh * A_HEAD_DIM:(hh + 1) * A_HEAD_DIM] = outs[hh]


def _band_bias(rel_bias):
    period = 3 * BAND_TQ
    d = jnp.arange(period)
    d = jnp.where(d > 2 * BAND_TQ, d - period, d)
    ridx = jnp.clip(BAND_TQ - d, -REL_CLIP_RIGHT, REL_CLIP_LEFT) + REL_CLIP_RIGHT
    base = rel_bias[:, ridx].astype(F32)
    heads = rel_bias.shape[0]
    toep = jnp.broadcast_to(base[:, None, :], (heads, BAND_TQ, period)).reshape(heads, BAND_TQ * period)
    toep = toep[:, :BAND_TQ * (period - 1)].reshape(heads, BAND_TQ, period - 1)[:, :, :2 * BAND_TQ]
    qc = (jnp.arange(BAND_TQ)[:, None] + BAND_TQ) // CHUNK
    kc = jnp.arange(2 * BAND_TQ)[None, :] // CHUNK
    valid = jnp.logical_and(kc <= qc, kc >= qc - A_LEFT_CHUNKS)
    return jnp.where(valid[None], toep * LOG2E, NEG)


def _band_attn(z, bias, bsz, seq):
    nq = seq // BAND_TQ
    width = BAND_HEADS * A_HEAD_DIM
    qb, kb, vb = QA0 // width, KA0 // width, VA0 // width
    cur = lambda c0: (lambda h, b, i: (b * nq + i, c0 + h))
    prev = lambda c0: (lambda h, b, i: (b * nq + jnp.maximum(i - 1, 0), c0 + h))
    blk = (BAND_TQ, width)
    return pl.pallas_call(
        _band_kernel,
        grid=(A_HEADS // BAND_HEADS, bsz, nq),
        in_specs=[pl.BlockSpec(blk, cur(qb)),
                  pl.BlockSpec(blk, prev(kb)), pl.BlockSpec(blk, cur(kb)),
                  pl.BlockSpec(blk, prev(vb)), pl.BlockSpec(blk, cur(vb)),
                  pl.BlockSpec((BAND_HEADS, BAND_TQ, 2 * BAND_TQ), lambda h, b, i: (h, 0, 0))],
        out_specs=pl.BlockSpec(blk, lambda h, b, i: (b * nq + i, h)),
        out_shape=jax.ShapeDtypeStruct((bsz * seq, A_WIDTH), BF16),
        compiler_params=_params("arbitrary", "arbitrary", "arbitrary"),
        name="band_attn",
    )(z, z, z, z, z, bias)


def _outproj_kernel(ya_ref, yb_ref, ga_ref, gb_ref, x_ref, g1_ref, sc2_ref, sh2_ref,
                    wpa_ref, wpb_ref, wo_ref, lg_ref, lb_ref, x1_ref, h2_ref):
    ta = jnp.dot(ya_ref[...], wpa_ref[...], preferred_element_type=F32)
    tb = jnp.dot(yb_ref[...], wpb_ref[...], preferred_element_type=F32)
    u = ga_ref[...].astype(F32) * ta + gb_ref[...].astype(F32) * tb
    y = jnp.dot(u.astype(BF16), wo_ref[...], preferred_element_type=F32)
    x1 = _layer_norm(DN_ALPHA * x_ref[...] + g1_ref[...] * y, lg_ref[...], lb_ref[...])
    x1_ref[...] = x1
    h2_ref[...] = (x1 * (1.0 + sc2_ref[...]) + sh2_ref[...]).astype(BF16)


def _out_proj(ya, yb, z, x2, mod4, wpa, wpb, wo, lg, lb, seq):
    t, d = x2.shape
    tm = 256
    tpb = seq // tm
    const = lambda i: (0, 0)
    modspec = lambda k: pl.BlockSpec((None, None, 1, d), lambda i: (i // tpb, k, 0, 0))
    return pl.pallas_call(
        _outproj_kernel,
        grid=(t // tm,),
        in_specs=[pl.BlockSpec((tm, A_WIDTH), lambda i: (i, 0)),
                  pl.BlockSpec((tm, B_WIDTH), lambda i: (i, 0)),
                  pl.BlockSpec((tm, d), lambda i: (i, GA0 // d)),
                  pl.BlockSpec((tm, d), lambda i: (i, GB0 // d)),
                  pl.BlockSpec((tm, d), lambda i: (i, 0)),
                  modspec(2), modspec(4), modspec(3),
                  pl.BlockSpec(wpa.shape, const), pl.BlockSpec(wpb.shape, const),
                  pl.BlockSpec(wo.shape, const),
                  pl.BlockSpec((1, d), const), pl.BlockSpec((1, d), const)],
        out_specs=[pl.BlockSpec((tm, d), lambda i: (i, 0)),
                   pl.BlockSpec((tm, d), lambda i: (i, 0))],
        out_shape=[jax.ShapeDtypeStruct((t, d), F32), jax.ShapeDtypeStruct((t, d), BF16)],
        compiler_params=_params("arbitrary"),
        name="out_proj",
    )(ya, yb, z, z, x2, mod4, mod4, mod4, wpa, wpb, wo, lg, lb)


TOPV = PEER_TOPK + 1
TOPV_ROWS = 24
COMB_DENSE = 8
COMB_ROWS = COMB_DENSE * PEER_TOPK + 2 * (TOPV_ROWS - COMB_DENSE)


def _top_vals(s, out_scr):
    for r in range(TOPV):
        m = jnp.max(s, axis=0, keepdims=True)
        out_scr[r:r + 1, :] = m
        if r + 1 < TOPV:
            s = jnp.where(s == m, -jnp.inf, s)


def _topk_kernel(h2_ref, wqt_ref, keys_ref, tau_ref, e1_ref, s2_ref, e2_ref,
                 st_scr, tv1_scr, tv2_scr, comb_scr, fv_scr):
    pqt = _nt_dot(wqt_ref[...], h2_ref[...]).astype(BF16)
    for hp in range(2 * PEER_HEADS):
        r0 = hp * PEER_HALF
        st_scr[r0:r0 + N_KEYS, :] = jnp.dot(keys_ref[hp], pqt[r0:r0 + PEER_HALF, :],
                                            preferred_element_type=F32)
    pad = jnp.full((TOPV_ROWS - PEER_TOPK, h2_ref.shape[0]), -jnp.inf, F32)
    tv1_scr[PEER_TOPK:, :] = pad
    tv2_scr[PEER_TOPK:, :] = pad

    def head(h, carry):
        r1 = pl.multiple_of(h * 2 * N_KEYS, 2 * N_KEYS)
        s1 = st_scr[pl.ds(r1, N_KEYS), :]
        s2 = st_scr[pl.ds(r1 + N_KEYS, N_KEYS), :]
        _top_vals(s1, tv1_scr)
        _top_vals(s2, tv2_scr)
        tv2 = tv2_scr[0:PEER_TOPK, :]
        for i in range(COMB_DENSE):
            comb_scr[i * PEER_TOPK:(i + 1) * PEER_TOPK, :] = tv1_scr[i:i + 1, :] + tv2
        c0 = COMB_DENSE * PEER_TOPK
        c1 = c0 + TOPV_ROWS - COMB_DENSE
        comb_scr[c0:c1, :] = tv1_scr[COMB_DENSE:, :] + tv2_scr[0:1, :]
        comb_scr[c1:, :] = tv2_scr[COMB_DENSE:, :] + tv1_scr[0:1, :]
        _top_vals(comb_scr[...], fv_scr)
        top = fv_scr[0:1, :]
        kept = fv_scr[0:PEER_TOPK, :]
        theta = 0.5 * (fv_scr[PEER_TOPK - 1:PEER_TOPK, :] + fv_scr[PEER_TOPK:PEER_TOPK + 1, :])
        inv_z = 1.0 / jnp.sum(jnp.exp(kept - top), axis=0, keepdims=True)
        tau_ref[h] = theta - s1
        e1_ref[h] = jnp.exp(s1 - tv1_scr[0:1, :]) * inv_z
        s2_ref[h] = s2
        e2_ref[h] = jnp.exp(s2 - tv2_scr[0:1, :])
        return carry

    lax.fori_loop(0, PEER_HEADS, head, 0)


def _peer_topk(h2, wqt, keys_b):
    t, d = h2.shape
    tm = 512
    rows = PEER_HEADS * 2 * N_KEYS
    out = jax.ShapeDtypeStruct((PEER_HEADS, N_KEYS, t), F32)
    ospec = pl.BlockSpec((PEER_HEADS, N_KEYS, tm), lambda i: (0, 0, i))
    return pl.pallas_call(
        _topk_kernel,
        grid=(t // tm,),
        in_specs=[pl.BlockSpec((tm, d), lambda i: (i, 0)),
                  pl.BlockSpec(wqt.shape, lambda i: (0, 0)),
                  pl.BlockSpec(keys_b.shape, lambda i: (0, 0, 0))],
        out_specs=[ospec] * 4,
        out_shape=[out] * 4,
        scratch_shapes=[pltpu.VMEM((rows, tm), F32),
                        pltpu.VMEM((TOPV_ROWS, tm), F32),
                        pltpu.VMEM((TOPV_ROWS, tm), F32),
                        pltpu.VMEM((COMB_ROWS, tm), F32),
                        pltpu.VMEM((TOPV_ROWS, tm), F32)],
        compiler_params=_params("arbitrary"),
        name="peer_topk",
    )(h2, wqt, keys_b)


PEER_TE = 1024
PEER_TM = 512
GATE_K1_TILE = 2


def _dense_stages(u_blk, vt_blk, h2_s, tau_s, e1_s, s2_s, e2_s, acc_s, at_new, at_old, wt_new, wt_old):
    acc_s[...] += jnp.dot(vt_blk[...], wt_old[...], preferred_element_type=F32)
    at_new[...] = _nt_dot(u_blk[...], h2_s[...])
    rows = SUBLANES_BF16
    for lb in range(PEER_TM // LANES):
        lanes = slice(lb * LANES, (lb + 1) * LANES)
        for g in range(N_KEYS // rows):
            krows = slice(g * rows, (g + 1) * rows)
            for k0 in range(0, PEER_TE // N_KEYS, GATE_K1_TILE):
                gates = [jnp.zeros((rows, LANES), F32) for _ in range(GATE_K1_TILE)]
                for h in range(PEER_HEADS):
                    s2 = s2_s[h, krows, lanes]
                    e2 = e2_s[h, krows, lanes]
                    for j in range(GATE_K1_TILE):
                        tau = tau_s[h, k0 + j:k0 + j + 1, lanes]
                        e1 = e1_s[h, k0 + j:k0 + j + 1, lanes]
                        gates[j] = gates[j] + jnp.where(s2 >= tau, e2, 0.0) * e1
                for j in range(GATE_K1_TILE):
                    r0 = (k0 + j) * N_KEYS + g * rows
                    a = at_old[r0:r0 + rows, lanes]
                    w = 0.5 * a * (1.0 + lax.erf(a * SQRT_HALF)) * gates[j]
                    wt_new[r0:r0 + rows, lanes] = w.astype(BF16)


def _dense_kernel(h2_ref, u_hbm, vt_hbm, tau_ref, e1_ref, s2_ref, e2_ref, pt_ref,
                  u_buf, vt_buf, u_sem, vt_sem, at0_scr, at1_scr, wt0_scr, wt1_scr,
                  acc_s, h2_s, tau_s, e1_s, s2_s, e2_s, *, n_e, n_blocks):
    s = pl.program_id(0)
    n_steps = n_blocks + 2
    blk = lambda step, lag: jnp.clip(step - lag, 0, n_blocks - 1)
    e_a = lax.rem(blk(s, 0), n_e)
    e_b = lax.rem(blk(s, 1), n_e)
    e_c = lax.rem(blk(s, 2), n_e)

    def u_copy(step, slot):
        rows = pl.ds(pl.multiple_of(lax.rem(blk(step, 0), n_e) * PEER_TE, PEER_TE), PEER_TE)
        return pltpu.make_async_copy(u_hbm.at[rows, :], u_buf.at[slot], u_sem.at[slot])

    def vt_copy(step, slot):
        cols = pl.ds(pl.multiple_of(lax.rem(blk(step, 2), n_e) * PEER_TE, PEER_TE), PEER_TE)
        return pltpu.make_async_copy(vt_hbm.at[:, cols], vt_buf.at[slot], vt_sem.at[slot])

    @pl.when(s == 0)
    def _():
        for scr in (at0_scr, at1_scr, wt0_scr, wt1_scr):
            scr[...] = jnp.zeros(scr.shape, scr.dtype)
        u_copy(s, 0).start()
        vt_copy(s, 0).start()

    @pl.when(e_a == 0)
    def _():
        h2_s[...] = h2_ref[...]

    @pl.when(e_b == 0)
    def _():
        s2_s[...] = s2_ref[...]
        e2_s[...] = e2_ref[...]

    @pl.when(e_c == 0)
    def _():
        acc_s[...] = jnp.zeros(acc_s.shape, F32)

    def step_body(slot, at_new, at_old, wt_new, wt_old):
        @pl.when(s + 1 < n_steps)
        def _():
            u_copy(s + 1, 1 - slot).start()
            vt_copy(s + 1, 1 - slot).start()

        u_copy(s, slot).wait()
        vt_copy(s, slot).wait()
        tau_s[...] = tau_ref[...]
        e1_s[...] = e1_ref[...]
        _dense_stages(u_buf.at[slot], vt_buf.at[slot], h2_s, tau_s, e1_s, s2_s, e2_s, acc_s,
                      at_new, at_old, wt_new, wt_old)

    @pl.when(lax.rem(s, 2) == 0)
    def _():
        step_body(0, at0_scr, at1_scr, wt1_scr, wt0_scr)

    @pl.when(lax.rem(s, 2) == 1)
    def _():
        step_body(1, at1_scr, at0_scr, wt0_scr, wt1_scr)

    @pl.when(jnp.logical_and(e_c == n_e - 1, s >= 2))
    def _():
        pt_ref[...] = acc_s[...]


def _peer_dense(h2, u_b, vt_b, tau, e1, s2, e2):
    t, d = h2.shape
    n_e = u_b.shape[0] // PEER_TE
    n_blocks = (t // PEER_TM) * n_e
    k1_blk = PEER_TE // N_KEYS
    blk = lambda lag: (lambda s: jnp.clip(s - lag, 0, n_blocks - 1))
    tile = lambda lag: (lambda s: blk(lag)(s) // n_e)
    eblk = lambda lag: (lambda s: lax.rem(blk(lag)(s), n_e))
    rspec = pl.BlockSpec((PEER_HEADS, k1_blk, PEER_TM), lambda s: (0, eblk(1)(s), tile(1)(s)))
    fspec = pl.BlockSpec((PEER_HEADS, N_KEYS, PEER_TM), lambda s: (0, 0, tile(1)(s)),
                         pipeline_mode=pl.Buffered(1))
    any_spec = pl.BlockSpec(memory_space=pl.ANY)
    return pl.pallas_call(
        functools.partial(_dense_kernel, n_e=n_e, n_blocks=n_blocks),
        grid=(n_blocks + 2,),
        in_specs=[pl.BlockSpec((PEER_TM, d), lambda s: (tile(0)(s), 0)),
                  any_spec, any_spec, rspec, rspec, fspec, fspec],
        out_specs=pl.BlockSpec((d, PEER_TM), lambda s: (0, tile(2)(s))),
        out_shape=jax.ShapeDtypeStruct((d, t), F32),
        scratch_shapes=[pltpu.VMEM((2, PEER_TE, d), BF16), pltpu.VMEM((2, d, PEER_TE), BF16),
                        pltpu.SemaphoreType.DMA((2,)), pltpu.SemaphoreType.DMA((2,)),
                        pltpu.VMEM((PEER_TE, PEER_TM), F32), pltpu.VMEM((PEER_TE, PEER_TM), F32),
                        pltpu.VMEM((PEER_TE, PEER_TM), BF16), pltpu.VMEM((PEER_TE, PEER_TM), BF16),
                        pltpu.VMEM((d, PEER_TM), F32), pltpu.VMEM((PEER_TM, d), BF16),
                        pltpu.VMEM((PEER_HEADS, k1_blk, PEER_TM), F32),
                        pltpu.VMEM((PEER_HEADS, k1_blk, PEER_TM), F32),
                        pltpu.VMEM((PEER_HEADS, N_KEYS, PEER_TM), F32),
                        pltpu.VMEM((PEER_HEADS, N_KEYS, PEER_TM), F32)],
        compiler_params=_params("arbitrary"),
        name="peer_dense",
    )(h2, u_b, vt_b, tau, e1, s2, e2)


def _final_kernel(pt_ref, x1_ref, g2_ref, lg_ref, lb_ref, o_ref):
    p = pt_ref[...].T
    o_ref[...] = _layer_norm(DN_ALPHA * x1_ref[...] + g2_ref[...] * p, lg_ref[...], lb_ref[...])


def _peer_final(pt, x1, mod4, lg, lb, seq):
    t, d = x1.shape
    tm = 512
    tpb = seq // tm
    return pl.pallas_call(
        _final_kernel,
        grid=(t // tm,),
        in_specs=[pl.BlockSpec((d, tm), lambda i: (0, i)),
                  pl.BlockSpec((tm, d), lambda i: (i, 0)),
                  pl.BlockSpec((None, None, 1, d), lambda i: (i // tpb, 5, 0, 0)),
                  pl.BlockSpec((1, d), lambda i: (0, 0)),
                  pl.BlockSpec((1, d), lambda i: (0, 0))],
        out_specs=pl.BlockSpec((tm, d), lambda i: (i, 0)),
        out_shape=jax.ShapeDtypeStruct((t, d), F32),
        compiler_params=_params("arbitrary"),
        name="peer_final",
    )(pt, x1, mod4, lg, lb)


def _half_swap(w):
    half = w.shape[-1] // 2
    return jnp.concatenate([w[..., half:], w[..., :half]], axis=-1)


def _in_proj_layout(w, pad_to):
    o = np.cumsum([0, A_WIDTH, A_WIDTH, A_WIDTH, Q_LORA, KV_LORA, QK_ROPE, pad_to, pad_to])
    qa, ka, va, cq, ckv, kr, ga, gb = [w[..., o[n]:o[n + 1]] for n in range(8)]
    z64 = jnp.zeros(kr.shape[:-1] + (LANES - QK_ROPE,), w.dtype)
    ztail = jnp.zeros(kr.shape[:-1] + (IN_TN - 2 * LANES,), w.dtype)
    return jnp.concatenate([ga, gb, qa, ka, va, cq, ckv, kr, z64, _half_swap(kr), z64, ztail], axis=-1)


def _mla_weight_layout(w_uq, w_ukv):
    r = w_uq.shape[0]
    uq = w_uq.reshape(r, B_HEADS, QK_NOPE + QK_ROPE)
    nope, rope = uq[..., :QK_NOPE], uq[..., QK_NOPE:]
    z64 = jnp.zeros((r, B_HEADS, B_HEAD_PAD - QK_NOPE - QK_ROPE), w_uq.dtype)
    wq1 = jnp.concatenate([nope, rope, z64], axis=-1).reshape(r, B_HEADS * B_HEAD_PAD)
    wq2 = jnp.concatenate([_half_swap(rope), z64], axis=-1).reshape(r, B_HEADS * LANES)
    ukv = w_ukv.reshape(w_ukv.shape[0], B_HEADS, QK_NOPE + V_HEAD)
    wk = ukv[..., :QK_NOPE].reshape(-1, B_HEADS * QK_NOPE)
    wv = ukv[..., QK_NOPE:].reshape(-1, B_WIDTH)
    return wq1.astype(BF16), wq2.astype(BF16), wk.astype(BF16), wv.astype(BF16)


def kernel(x, c, positions, w_ada, b_ada, w_in, b_in, rel_bias, q_norm_g, kv_norm_g, w_uq, w_ukv,
           w_pa, w_pb, w_o, ln1_g, ln1_b, peer_wq, peer_keys, peer_u, peer_v, ln2_g, ln2_b):
    bsz, seq, d = x.shape
    t = bsz * seq
    assert d == GB0 - GA0 and seq % 1024 == 0 and w_ada.shape[0] == 1
    x2 = x.reshape(t, d)
    row = lambda a: a.reshape(1, -1).astype(F32)

    mod4 = _ada_mod(c, w_ada[0], b_ada[0]).reshape(bsz, 6, 1, d)
    cos_t, sin_t = _rope_tables(positions)

    w2 = _in_proj_layout(w_in[0], d).astype(BF16)
    b2 = _in_proj_layout(b_in[0], d).reshape(1, Z_WIDTH).astype(F32)
    z = _in_proj(x2, mod4, w2, b2, row(q_norm_g[0]), row(kv_norm_g[0]), seq)

    wq1, wq2, wk, wv = _mla_weight_layout(w_uq[0], w_ukv[0])
    qb, kb, vb = _mla_up(z, cos_t, sin_t, wq1, wq2, wk, wv)
    yb = _mla_flash(qb, kb, vb, bsz, seq)
    ya = _band_attn(z, _band_bias(rel_bias[0]), bsz, seq)

    x1, h2 = _out_proj(ya, yb, z, x2, mod4, w_pa[0].astype(BF16), w_pb[0].astype(BF16),
                       w_o[0].astype(BF16), row(ln1_g[0]), row(ln1_b[0]), seq)

    wqt = peer_wq[0].astype(BF16).T
    keys_b = peer_keys[0].reshape(2 * PEER_HEADS, N_KEYS, PEER_HALF).astype(BF16)
    tau, e1, s2, e2 = _peer_topk(h2, wqt, keys_b)
    pt = _peer_dense(h2, peer_u[0].astype(BF16), peer_v[0].astype(BF16).T, tau, e1, s2, e2)
    out = _peer_final(pt, x1, mod4, row(ln2_g[0]), row(ln2_b[0]), seq)
    return out.reshape(bsz, seq, d)
```

```python
import functools
import math

import jax
import jax.numpy as jnp
import numpy as np
from jax import lax
from jax.experimental import pallas as pl
from jax.experimental.pallas import tpu as pltpu

F32 = jnp.float32
BF16 = jnp.bfloat16

CHUNK = 64
A_HEADS = 8
A_HEAD_DIM = 128
A_LEFT_CHUNKS = 8
REL_CLIP_LEFT = 256
REL_CLIP_RIGHT = CHUNK - 1
B_HEADS = 8
Q_LORA = 512
KV_LORA = 512
QK_NOPE = 128
QK_ROPE = 64
V_HEAD = 128
ROPE_THETA = 10000.0
N_KEYS = 128
PEER_HEADS = 8
PEER_TOPK = 16
PEER_HALF = 128
DEPTH = 1
DN_ALPHA = (2 * DEPTH) ** 0.25
LN_EPS = 1e-5
RMS_EPS = 1e-6
NEG = -1e30

A_WIDTH = A_HEADS * A_HEAD_DIM
B_WIDTH = B_HEADS * V_HEAD
LOG2E = math.log2(math.e)
A_QSCALE = A_HEAD_DIM ** -0.5 * LOG2E
B_QSCALE = (QK_NOPE + QK_ROPE) ** -0.5 * LOG2E
SQRT_HALF = math.sqrt(0.5)

LANES = 128
SUBLANES_BF16 = 16
VMEM_LIMIT_BYTES = 56 * 1024 * 1024

IN_TN = 512
GA0 = 0
GB0 = 2048
QA0 = 4096
KA0 = 5120
VA0 = 6144
CQ0 = 7168
CKV0 = 7680
KR0 = 8192
Z_WIDTH = 8704
B_HEAD_PAD = 256


def _nt_dot(a, b):
    return lax.dot_general(a, b, (((1,), (1,)), ((), ())), preferred_element_type=F32)


def _params(*sem, flags=None, vmem=VMEM_LIMIT_BYTES):
    return pltpu.CompilerParams(dimension_semantics=sem, vmem_limit_bytes=vmem, flags=flags)


def _layer_norm(r, g, b):
    mu = jnp.mean(r, axis=-1, keepdims=True)
    d = r - mu
    var = jnp.mean(d * d, axis=-1, keepdims=True)
    return d * lax.rsqrt(var + LN_EPS) * g + b


def _ada_kernel(c_ref, w_ref, b_ref, o_ref):
    c = c_ref[...]
    cond = (c * jax.nn.sigmoid(c)).astype(BF16)
    o_ref[...] = jnp.dot(cond, w_ref[...].astype(BF16), preferred_element_type=F32) + b_ref[...]


def _ada_mod(c, w_ada, b_ada):
    bsz, d = c.shape
    n = w_ada.shape[1]
    tn = 1024
    return pl.pallas_call(
        _ada_kernel,
        grid=(n // tn,),
        in_specs=[pl.BlockSpec((bsz, d), lambda j: (0, 0)),
                  pl.BlockSpec((d, tn), lambda j: (0, j)),
                  pl.BlockSpec((1, tn), lambda j: (0, j))],
        out_specs=pl.BlockSpec((bsz, tn), lambda j: (0, j)),
        out_shape=jax.ShapeDtypeStruct((bsz, n), F32),
        compiler_params=_params("arbitrary"),
        name="ada_mod",
    )(c, w_ada, b_ada.reshape(1, n))


def _rope_kernel(pos_ref, invf_ref, cos_ref, sin_ref):
    ang = pos_ref[...] * invf_ref[...]
    cos_ref[...] = jnp.cos(ang)
    sin_ref[...] = jnp.sin(ang)


def _rope_tables(positions):
    t = positions.size
    half = QK_ROPE // 2
    per_row = LANES // half
    rows = t // per_row
    inv_freq = ROPE_THETA ** (-jnp.arange(0, QK_ROPE, 2, dtype=F32) / QK_ROPE)
    pos_rep = jnp.repeat(positions.astype(F32).reshape(rows, per_row), half, axis=1)
    invf = jnp.tile(inv_freq, per_row).reshape(1, LANES)
    tr = min(rows, 2048)
    cos, sin = pl.pallas_call(
        _rope_kernel,
        grid=(rows // tr,),
        in_specs=[pl.BlockSpec((tr, LANES), lambda i: (i, 0)),
                  pl.BlockSpec((1, LANES), lambda i: (0, 0))],
        out_specs=[pl.BlockSpec((tr, LANES), lambda i: (i, 0))] * 2,
        out_shape=[jax.ShapeDtypeStruct((rows, LANES), F32)] * 2,
        compiler_params=_params("arbitrary"),
        name="rope_tab",
    )(pos_rep, invf)
    cos = cos.reshape(t, half)
    sin = sin.reshape(t, half)
    zeros = jnp.zeros((t, LANES - QK_ROPE), F32)
    return (jnp.concatenate([cos, cos, zeros], axis=1),
            jnp.concatenate([-sin, sin, zeros], axis=1))


def _inproj_kernel(x_ref, sc_ref, sh_ref, w_ref, b_ref, gq_ref, gkv_ref, z_ref, h_scr):
    j = pl.program_id(1)

    @pl.when(j == 0)
    def _():
        h_scr[...] = (x_ref[...] * (1.0 + sc_ref[...]) + sh_ref[...]).astype(BF16)

    acc = jnp.dot(h_scr[...], w_ref[...], preferred_element_type=F32) + b_ref[...]

    def rms(g):
        ms = jnp.mean(acc * acc, axis=-1, keepdims=True)
        return acc * lax.rsqrt(ms + RMS_EPS) * g

    @pl.when(j < QA0 // IN_TN)
    def _():
        z_ref[...] = jax.nn.sigmoid(acc).astype(BF16)

    @pl.when(jnp.logical_and(j >= QA0 // IN_TN, j < KA0 // IN_TN))
    def _():
        z_ref[...] = (acc * A_QSCALE).astype(BF16)

    @pl.when(jnp.logical_or(jnp.logical_and(j >= KA0 // IN_TN, j < CQ0 // IN_TN), j == KR0 // IN_TN))
    def _():
        z_ref[...] = acc.astype(BF16)

    @pl.when(j == CQ0 // IN_TN)
    def _():
        z_ref[...] = rms(gq_ref[...]).astype(BF16)

    @pl.when(j == CKV0 // IN_TN)
    def _():
        z_ref[...] = rms(gkv_ref[...]).astype(BF16)


def _in_proj(x2, mod4, w2, b2, gq, gkv, seq):
    t, d = x2.shape
    tm = min(1024, seq)
    tpb = seq // tm
    return pl.pallas_call(
        _inproj_kernel,
        grid=(t // tm, Z_WIDTH // IN_TN),
        in_specs=[pl.BlockSpec((tm, d), lambda i, j: (i, 0)),
                  pl.BlockSpec((None, None, 1, d), lambda i, j: (i // tpb, 1, 0, 0)),
                  pl.BlockSpec((None, None, 1, d), lambda i, j: (i // tpb, 0, 0, 0)),
                  pl.BlockSpec((d, IN_TN), lambda i, j: (0, j)),
                  pl.BlockSpec((1, IN_TN), lambda i, j: (0, j)),
                  pl.BlockSpec((1, Q_LORA), lambda i, j: (0, 0)),
                  pl.BlockSpec((1, KV_LORA), lambda i, j: (0, 0))],
        out_specs=pl.BlockSpec((tm, IN_TN), lambda i, j: (i, j)),
        out_shape=jax.ShapeDtypeStruct((t, Z_WIDTH), BF16),
        scratch_shapes=[pltpu.VMEM((tm, d), BF16)],
        compiler_params=_params("arbitrary", "arbitrary"),
        name="in_proj",
    )(x2, mod4, mod4, w2, b2, gq, gkv)


def _mla_up_kernel(cq_ref, ckv_ref, kra_ref, krb_ref, cos_ref, sin_ref,
                   wq1_ref, wq2_ref, wk_ref, wv_ref, q_ref, k_ref, v_ref):
    cq = cq_ref[...]
    ckv = ckv_ref[...]
    cos = cos_ref[...]
    sin = sin_ref[...]
    qa = jnp.dot(cq, wq1_ref[...], preferred_element_type=F32)
    qb = jnp.dot(cq, wq2_ref[...], preferred_element_type=F32)
    kn = jnp.dot(ckv, wk_ref[...], preferred_element_type=F32)
    v_ref[...] = jnp.dot(ckv, wv_ref[...], preferred_element_type=F32).astype(BF16)
    k_rope = (kra_ref[...].astype(F32) * cos + krb_ref[...].astype(F32) * sin).astype(BF16)
    for h in range(B_HEADS):
        c0 = h * B_HEAD_PAD
        n0 = h * QK_NOPE
        q_ref[:, c0:c0 + QK_NOPE] = (qa[:, c0:c0 + QK_NOPE] * B_QSCALE).astype(BF16)
        q_rope = qa[:, c0 + QK_NOPE:c0 + B_HEAD_PAD] * cos + qb[:, n0:n0 + QK_NOPE] * sin
        q_ref[:, c0 + QK_NOPE:c0 + B_HEAD_PAD] = (q_rope * B_QSCALE).astype(BF16)
        k_ref[:, c0:c0 + QK_NOPE] = kn[:, n0:n0 + QK_NOPE].astype(BF16)
        k_ref[:, c0 + QK_NOPE:c0 + B_HEAD_PAD] = k_rope


def _mla_up(z, cos_t, sin_t, wq1, wq2, wk, wv):
    t = z.shape[0]
    tm = 512
    const = lambda i: (0, 0)
    return pl.pallas_call(
        _mla_up_kernel,
        grid=(t // tm,),
        in_specs=[pl.BlockSpec((tm, Q_LORA), lambda i: (i, CQ0 // Q_LORA)),
                  pl.BlockSpec((tm, KV_LORA), lambda i: (i, CKV0 // KV_LORA)),
                  pl.BlockSpec((tm, LANES), lambda i: (i, KR0 // LANES)),
                  pl.BlockSpec((tm, LANES), lambda i: (i, KR0 // LANES + 1)),
                  pl.BlockSpec((tm, LANES), lambda i: (i, 0)),
                  pl.BlockSpec((tm, LANES), lambda i: (i, 0)),
                  pl.BlockSpec(wq1.shape, const),
                  pl.BlockSpec(wq2.shape, const),
                  pl.BlockSpec(wk.shape, const),
                  pl.BlockSpec(wv.shape, const)],
        out_specs=[pl.BlockSpec((tm, B_HEADS * B_HEAD_PAD), lambda i: (i, 0)),
                   pl.BlockSpec((tm, B_HEADS * B_HEAD_PAD), lambda i: (i, 0)),
                   pl.BlockSpec((tm, B_WIDTH), lambda i: (i, 0))],
        out_shape=[jax.ShapeDtypeStruct((t, B_HEADS * B_HEAD_PAD), BF16),
                   jax.ShapeDtypeStruct((t, B_HEADS * B_HEAD_PAD), BF16),
                   jax.ShapeDtypeStruct((t, B_WIDTH), BF16)],
        compiler_params=_params("arbitrary"),
        name="mla_up",
    )(z, z, z, z, cos_t, sin_t, wq1, wq2, wk, wv)


FLASH_HEADS = 2


def _flash_kernel(qi_ref, kj_ref, q_ref, k_ref, v_ref, o_ref, q_s, k_s, v_s, m_scr, l_scr, acc_scr):
    p = pl.program_id(2)
    i = qi_ref[p]
    j = kj_ref[p]
    tq = q_ref.shape[0]
    tk = k_ref.shape[0]

    @pl.when(j == 0)
    def _():
        m_scr[...] = jnp.full(m_scr.shape, -jnp.inf, F32)
        l_scr[...] = jnp.zeros(l_scr.shape, F32)
        acc_scr[...] = jnp.zeros(acc_scr.shape, F32)

    def step(diagonal):
        q_s[...] = q_ref[...]
        k_s[...] = k_ref[...]
        v_s[...] = v_ref[...]
        for hh in range(FLASH_HEADS):
            qk = slice(hh * B_HEAD_PAD, (hh + 1) * B_HEAD_PAD)
            vv = slice(hh * V_HEAD, (hh + 1) * V_HEAD)
            s = _nt_dot(q_s[:, qk], k_s[:, qk])
            if diagonal:
                qpos = lax.broadcasted_iota(jnp.int32, (tq, tk), 0)
                kpos = lax.broadcasted_iota(jnp.int32, (tq, tk), 1)
                s = jnp.where(kpos <= (qpos | (CHUNK - 1)), s, NEG)
            m_prev = m_scr[hh]
            m_new = jnp.maximum(m_prev, jnp.max(s, axis=1, keepdims=True))
            alpha = jnp.exp2(m_prev - m_new)
            pexp = jnp.exp2(s - m_new)
            l_new = alpha * l_scr[hh] + jnp.sum(pexp, axis=1, keepdims=True)
            acc = alpha * acc_scr[hh] + jnp.dot(pexp.astype(BF16), v_s[:, vv],
                                                preferred_element_type=F32)
            m_scr[hh] = m_new
            l_scr[hh] = l_new
            acc_scr[hh] = acc
            if diagonal:
                o_ref[:, vv] = (acc / l_new).astype(BF16)

    @pl.when(j < i)
    def _():
        step(False)

    @pl.when(j == i)
    def _():
        step(True)


def _mla_flash(q, k, v, bsz, seq):
    tq = min(1024, seq)
    nqb = seq // tq
    pairs = [(i, j) for i in range(nqb) for j in range(i + 1)]
    qi = jnp.asarray([p[0] for p in pairs], jnp.int32)
    kj = jnp.asarray([p[1] for p in pairs], jnp.int32)
    qk_w = FLASH_HEADS * B_HEAD_PAD
    v_w = FLASH_HEADS * V_HEAD
    grid_spec = pltpu.PrefetchScalarGridSpec(
        num_scalar_prefetch=2,
        grid=(bsz, B_HEADS // FLASH_HEADS, len(pairs)),
        in_specs=[pl.BlockSpec((tq, qk_w), lambda b, h, p, qi, kj: (b * nqb + qi[p], h)),
                  pl.BlockSpec((tq, qk_w), lambda b, h, p, qi, kj: (b * nqb + kj[p], h)),
                  pl.BlockSpec((tq, v_w), lambda b, h, p, qi, kj: (b * nqb + kj[p], h))],
        out_specs=pl.BlockSpec((tq, v_w), lambda b, h, p, qi, kj: (b * nqb + qi[p], h)),
        scratch_shapes=[pltpu.VMEM((tq, qk_w), BF16), pltpu.VMEM((tq, qk_w), BF16),
                        pltpu.VMEM((tq, v_w), BF16),
                        pltpu.VMEM((FLASH_HEADS, tq, 1), F32), pltpu.VMEM((FLASH_HEADS, tq, 1), F32),
                        pltpu.VMEM((FLASH_HEADS, tq, V_HEAD), F32)],
    )
    return pl.pallas_call(
        _flash_kernel,
        grid_spec=grid_spec,
        out_shape=jax.ShapeDtypeStruct((bsz * seq, B_WIDTH), BF16),
        compiler_params=_params("arbitrary", "arbitrary", "arbitrary"),
        name="mla_flash",
    )(qi, kj, q, k, v)


BAND_TQ = A_LEFT_CHUNKS * CHUNK


BAND_HEADS = 2


def _band_kernel(q_ref, kp_ref, kc_ref, vp_ref, vc_ref, bias_ref, o_ref):
    i = pl.program_id(2)
    first = jnp.where(i == 0, NEG, 0.0).astype(F32)
    outs = []
    for hh in range(BAND_HEADS):
        cols = slice(hh * A_HEAD_DIM, (hh + 1) * A_HEAD_DIM)
        q = q_ref[:, cols]
        s_prev = _nt_dot(q, kp_ref[:, cols]) + bias_ref[hh, :, :BAND_TQ] + first
        s_cur = _nt_dot(q, kc_ref[:, cols]) + bias_ref[hh, :, BAND_TQ:]
        m = jnp.maximum(jnp.max(s_prev, axis=1, keepdims=True), jnp.max(s_cur, axis=1, keepdims=True))
        p_prev = jnp.exp2(s_prev - m)
        p_cur = jnp.exp2(s_cur - m)
        l = jnp.sum(p_prev, axis=1, keepdims=True) + jnp.sum(p_cur, axis=1, keepdims=True)
        o = (jnp.dot(p_prev.astype(BF16), vp_ref[:, cols], preferred_element_type=F32)
             + jnp.dot(p_cur.astype(BF16), vc_ref[:, cols], preferred_element_type=F32))
        outs.append((o / l).astype(BF16))
    for hh in range(BAND_HEADS):
        o_ref[:, hh * A_HEAD_DIM:(hh + 1) * A_HEAD_DIM] = outs[hh]


def _band_bias(rel_bias):
    period = 3 * BAND_TQ
    n_rel = rel_bias.shape[1]
    rep = lambda col, n: jnp.broadcast_to(rel_bias[:, col:col + 1], (rel_bias.shape[0], n))
    n_hi = BAND_TQ - REL_CLIP_LEFT
    base = jnp.concatenate([rep(n_rel - 1, n_hi), rel_bias[:, ::-1],
                            rep(0, 2 * BAND_TQ + 1 - n_hi - n_rel), rep(n_rel - 1, BAND_TQ - 1)],
                           axis=1).astype(F32)
    heads = rel_bias.shape[0]
    toep = jnp.broadcast_to(base[:, None, :], (heads, BAND_TQ, period)).reshape(heads, BAND_TQ * period)
    toep = toep[:, :BAND_TQ * (period - 1)].reshape(heads, BAND_TQ, period - 1)[:, :, :2 * BAND_TQ]
    qc = (jnp.arange(BAND_TQ)[:, None] + BAND_TQ) // CHUNK
    kc = jnp.arange(2 * BAND_TQ)[None, :] // CHUNK
    valid = jnp.logical_and(kc <= qc, kc >= qc - A_LEFT_CHUNKS)
    return jnp.where(valid[None], toep * LOG2E, NEG)


def _band_attn(z, bias, bsz, seq):
    nq = seq // BAND_TQ
    width = BAND_HEADS * A_HEAD_DIM
    qb, kb, vb = QA0 // width, KA0 // width, VA0 // width
    cur = lambda c0: (lambda h, b, i: (b * nq + i, c0 + h))
    prev = lambda c0: (lambda h, b, i: (b * nq + jnp.maximum(i - 1, 0), c0 + h))
    blk = (BAND_TQ, width)
    return pl.pallas_call(
        _band_kernel,
        grid=(A_HEADS // BAND_HEADS, bsz, nq),
        in_specs=[pl.BlockSpec(blk, cur(qb)),
                  pl.BlockSpec(blk, prev(kb)), pl.BlockSpec(blk, cur(kb)),
                  pl.BlockSpec(blk, prev(vb)), pl.BlockSpec(blk, cur(vb)),
                  pl.BlockSpec((BAND_HEADS, BAND_TQ, 2 * BAND_TQ), lambda h, b, i: (h, 0, 0))],
        out_specs=pl.BlockSpec(blk, lambda h, b, i: (b * nq + i, h)),
        out_shape=jax.ShapeDtypeStruct((bsz * seq, A_WIDTH), BF16),
        compiler_params=_params("arbitrary", "arbitrary", "arbitrary"),
        name="band_attn",
    )(z, z, z, z, z, bias)


def _outproj_kernel(ya_ref, yb_ref, ga_ref, gb_ref, x_ref, g1_ref, sc2_ref, sh2_ref,
                    wpa_ref, wpb_ref, wo_ref, lg_ref, lb_ref, x1_ref, h2_ref):
    ta = jnp.dot(ya_ref[...], wpa_ref[...], preferred_element_type=F32)
    tb = jnp.dot(yb_ref[...], wpb_ref[...], preferred_element_type=F32)
    u = ga_ref[...].astype(F32) * ta + gb_ref[...].astype(F32) * tb
    y = jnp.dot(u.astype(BF16), wo_ref[...], preferred_element_type=F32)
    x1 = _layer_norm(DN_ALPHA * x_ref[...] + g1_ref[...] * y, lg_ref[...], lb_ref[...])
    x1_ref[...] = x1
    h2_ref[...] = (x1 * (1.0 + sc2_ref[...]) + sh2_ref[...]).astype(BF16)


def _out_proj(ya, yb, z, x2, mod4, wpa, wpb, wo, lg, lb, seq):
    t, d = x2.shape
    tm = 256
    tpb = seq // tm
    const = lambda i: (0, 0)
    modspec = lambda k: pl.BlockSpec((None, None, 1, d), lambda i: (i // tpb, k, 0, 0))
    return pl.pallas_call(
        _outproj_kernel,
        grid=(t // tm,),
        in_specs=[pl.BlockSpec((tm, A_WIDTH), lambda i: (i, 0)),
                  pl.BlockSpec((tm, B_WIDTH), lambda i: (i, 0)),
                  pl.BlockSpec((tm, d), lambda i: (i, GA0 // d)),
                  pl.BlockSpec((tm, d), lambda i: (i, GB0 // d)),
                  pl.BlockSpec((tm, d), lambda i: (i, 0)),
                  modspec(2), modspec(4), modspec(3),
                  pl.BlockSpec(wpa.shape, const), pl.BlockSpec(wpb.shape, const),
                  pl.BlockSpec(wo.shape, const),
                  pl.BlockSpec((1, d), const), pl.BlockSpec((1, d), const)],
        out_specs=[pl.BlockSpec((tm, d), lambda i: (i, 0)),
                   pl.BlockSpec((tm, d), lambda i: (i, 0))],
        out_shape=[jax.ShapeDtypeStruct((t, d), F32), jax.ShapeDtypeStruct((t, d), BF16)],
        compiler_params=_params("arbitrary"),
        name="out_proj",
    )(ya, yb, z, z, x2, mod4, mod4, mod4, wpa, wpb, wo, lg, lb)


TOPV = PEER_TOPK + 1
TOPV_ROWS = 24
SUBLANES_F32 = 8


def _sort_network(n):
    def merge(lo, hi, r):
        step = r * 2
        if step < hi - lo:
            yield from merge(lo, hi, step)
            yield from merge(lo + r, hi, step)
            yield from [(i, i + r) for i in range(lo + r, hi - r, step)]
        else:
            yield (lo, lo + r)

    def sort(lo, hi):
        if hi - lo >= 1:
            mid = lo + (hi - lo) // 2
            yield from sort(lo, mid)
            yield from sort(mid + 1, hi)
            yield from merge(lo, hi, 1)

    return list(sort(0, n - 1))


def _pop_top(lists, singles, out_scr):
    depth = len(lists)
    for r in range(TOPV):
        head = lists[0]
        for x in singles:
            head = jnp.maximum(head, x)
        m = jnp.max(head, axis=0, keepdims=True)
        out_scr[r:r + 1, :] = m
        remaining = TOPV - 1 - r
        if remaining == 0:
            break
        hit = lists[0] == m
        for k in range(min(depth, remaining)):
            nxt = lists[k + 1] if k + 1 < depth else -jnp.inf
            lists[k] = jnp.where(hit, nxt, lists[k])
        singles = [jnp.where(x == m, -jnp.inf, x) for x in singles]


def _top_vals(s, out_scr):
    n = s.shape[0] // SUBLANES_F32
    lists = [s[k * SUBLANES_F32:(k + 1) * SUBLANES_F32, :] for k in range(n)]
    for a, b in _sort_network(n):
        lists[a], lists[b] = jnp.maximum(lists[a], lists[b]), jnp.minimum(lists[a], lists[b])
    _pop_top(lists, [], out_scr)


def _topk_kernel(h2_ref, wqt_ref, keys_ref, tau_ref, e1_ref, s2_ref, e2_ref,
                 st_scr, tv1_scr, tv2_scr, fv_scr):
    pqt = _nt_dot(wqt_ref[...], h2_ref[...]).astype(BF16)
    for hp in range(2 * PEER_HEADS):
        r0 = hp * PEER_HALF
        st_scr[r0:r0 + N_KEYS, :] = jnp.dot(keys_ref[hp], pqt[r0:r0 + PEER_HALF, :],
                                            preferred_element_type=F32)
    pad = jnp.full((TOPV_ROWS - PEER_TOPK, h2_ref.shape[0]), -jnp.inf, F32)
    tv1_scr[PEER_TOPK:, :] = pad
    tv2_scr[PEER_TOPK:, :] = pad

    def head(h, carry):
        r1 = pl.multiple_of(h * 2 * N_KEYS, 2 * N_KEYS)
        s1 = st_scr[pl.ds(r1, N_KEYS), :]
        s2 = st_scr[pl.ds(r1 + N_KEYS, N_KEYS), :]
        _top_vals(s1, tv1_scr)
        _top_vals(s2, tv2_scr)
        tv1_lo = tv1_scr[0:SUBLANES_F32, :]
        lists = [tv1_lo + tv2_scr[j:j + 1, :] for j in range(PEER_TOPK)]
        singles = [tv1_scr[SUBLANES_F32:PEER_TOPK, :] + tv2_scr[0:1, :],
                   tv1_scr[PEER_TOPK:, :] + tv2_scr[0:1, :],
                   tv2_scr[PEER_TOPK:, :] + tv1_scr[0:1, :]]
        _pop_top(lists, singles, fv_scr)
        top = fv_scr[0:1, :]
        kept = fv_scr[0:PEER_TOPK, :]
        theta = 0.5 * (fv_scr[PEER_TOPK - 1:PEER_TOPK, :] + fv_scr[PEER_TOPK:PEER_TOPK + 1, :])
        inv_z = 1.0 / jnp.sum(jnp.exp(kept - top), axis=0, keepdims=True)
        tau_ref[h] = theta - s1
        e1_ref[h] = jnp.exp(s1 - tv1_scr[0:1, :]) * inv_z
        s2_ref[h] = s2
        e2_ref[h] = jnp.exp(s2 - tv2_scr[0:1, :])
        return carry

    lax.fori_loop(0, PEER_HEADS, head, 0)


def _peer_topk(h2, wqt, keys_b):
    t, d = h2.shape
    tm = 512
    rows = PEER_HEADS * 2 * N_KEYS
    out = jax.ShapeDtypeStruct((PEER_HEADS, N_KEYS, t), F32)
    ospec = pl.BlockSpec((PEER_HEADS, N_KEYS, tm), lambda i: (0, 0, i))
    return pl.pallas_call(
        _topk_kernel,
        grid=(t // tm,),
        in_specs=[pl.BlockSpec((tm, d), lambda i: (i, 0)),
                  pl.BlockSpec(wqt.shape, lambda i: (0, 0)),
                  pl.BlockSpec(keys_b.shape, lambda i: (0, 0, 0))],
        out_specs=[ospec] * 4,
        out_shape=[out] * 4,
        scratch_shapes=[pltpu.VMEM((rows, tm), F32),
                        pltpu.VMEM((TOPV_ROWS, tm), F32),
                        pltpu.VMEM((TOPV_ROWS, tm), F32),
                        pltpu.VMEM((TOPV_ROWS, tm), F32)],
        compiler_params=_params("arbitrary"),
        name="peer_topk",
    )(h2, wqt, keys_b)


PEER_TE = 1024
PEER_TM = 512
GATE_K1_TILE = 2


def _dense_stages(u_blk, vt_blk, h2_s, tau_s, e1_s, s2_s, e2_s, acc_s, at_new, at_old, wt_new, wt_old):
    acc_s[...] += jnp.dot(vt_blk[...], wt_old[...], preferred_element_type=F32)
    at_new[...] = _nt_dot(u_blk[...], h2_s[...])
    rows = SUBLANES_BF16
    for lb in range(PEER_TM // LANES):
        lanes = slice(lb * LANES, (lb + 1) * LANES)
        for g in range(N_KEYS // rows):
            krows = slice(g * rows, (g + 1) * rows)
            for k0 in range(0, PEER_TE // N_KEYS, GATE_K1_TILE):
                gates = [jnp.zeros((rows, LANES), F32) for _ in range(GATE_K1_TILE)]
                for h in range(PEER_HEADS):
                    s2 = s2_s[h, krows, lanes]
                    e2 = e2_s[h, krows, lanes]
                    for j in range(GATE_K1_TILE):
                        tau = tau_s[h, k0 + j:k0 + j + 1, lanes]
                        e1 = e1_s[h, k0 + j:k0 + j + 1, lanes]
                        gates[j] = gates[j] + jnp.where(s2 >= tau, e2, 0.0) * e1
                for j in range(GATE_K1_TILE):
                    r0 = (k0 + j) * N_KEYS + g * rows
                    a = at_old[r0:r0 + rows, lanes]
                    w = 0.5 * a * (1.0 + lax.erf(a * SQRT_HALF)) * gates[j]
                    wt_new[r0:r0 + rows, lanes] = w.astype(BF16)


def _dense_kernel(h2_ref, u_hbm, vt_hbm, tau_ref, e1_ref, s2_ref, e2_ref, pt_ref,
                  u_buf, vt_buf, u_sem, vt_sem, at0_scr, at1_scr, wt0_scr, wt1_scr,
                  acc_s, h2_s, tau_s, e1_s, s2_s, e2_s, *, n_e, n_blocks):
    s = pl.program_id(0)
    n_steps = n_blocks + 2
    blk = lambda step, lag: jnp.clip(step - lag, 0, n_blocks - 1)
    e_a = lax.rem(blk(s, 0), n_e)
    e_b = lax.rem(blk(s, 1), n_e)
    e_c = lax.rem(blk(s, 2), n_e)

    def u_copy(step, slot):
        rows = pl.ds(pl.multiple_of(lax.rem(blk(step, 0), n_e) * PEER_TE, PEER_TE), PEER_TE)
        return pltpu.make_async_copy(u_hbm.at[rows, :], u_buf.at[slot], u_sem.at[slot])

    def vt_copy(step, slot):
        cols = pl.ds(pl.multiple_of(lax.rem(blk(step, 2), n_e) * PEER_TE, PEER_TE), PEER_TE)
        return pltpu.make_async_copy(vt_hbm.at[:, cols], vt_buf.at[slot], vt_sem.at[slot])

    @pl.when(s == 0)
    def _():
        for scr in (at0_scr, at1_scr, wt0_scr, wt1_scr):
            scr[...] = jnp.zeros(scr.shape, scr.dtype)
        u_copy(s, 0).start()
        vt_copy(s, 0).start()

    @pl.when(e_a == 0)
    def _():
        h2_s[...] = h2_ref[...]

    @pl.when(e_b == 0)
    def _():
        s2_s[...] = s2_ref[...]
        e2_s[...] = e2_ref[...]

    @pl.when(e_c == 0)
    def _():
        acc_s[...] = jnp.zeros(acc_s.shape, F32)

    def step_body(slot, at_new, at_old, wt_new, wt_old):
        @pl.when(s + 1 < n_steps)
        def _():
            u_copy(s + 1, 1 - slot).start()
            vt_copy(s + 1, 1 - slot).start()

        u_copy(s, slot).wait()
        vt_copy(s, slot).wait()
        tau_s[...] = tau_ref[...]
        e1_s[...] = e1_ref[...]
        _dense_stages(u_buf.at[slot], vt_buf.at[slot], h2_s, tau_s, e1_s, s2_s, e2_s, acc_s,
                      at_new, at_old, wt_new, wt_old)

    @pl.when(lax.rem(s, 2) == 0)
    def _():
        step_body(0, at0_scr, at1_scr, wt1_scr, wt0_scr)

    @pl.when(lax.rem(s, 2) == 1)
    def _():
        step_body(1, at1_scr, at0_scr, wt0_scr, wt1_scr)

    @pl.when(jnp.logical_and(e_c == n_e - 1, s >= 2))
    def _():
        pt_ref[...] = acc_s[...]


def _peer_dense(h2, u_b, vt_b, tau, e1, s2, e2):
    t, d = h2.shape
    n_e = u_b.shape[0] // PEER_TE
    n_blocks = (t // PEER_TM) * n_e
    k1_blk = PEER_TE // N_KEYS
    blk = lambda lag: (lambda s: jnp.clip(s - lag, 0, n_blocks - 1))
    tile = lambda lag: (lambda s: blk(lag)(s) // n_e)
    eblk = lambda lag: (lambda s: lax.rem(blk(lag)(s), n_e))
    rspec = pl.BlockSpec((PEER_HEADS, k1_blk, PEER_TM), lambda s: (0, eblk(1)(s), tile(1)(s)))
    fspec = pl.BlockSpec((PEER_HEADS, N_KEYS, PEER_TM), lambda s: (0, 0, tile(1)(s)),
                         pipeline_mode=pl.Buffered(1))
    any_spec = pl.BlockSpec(memory_space=pl.ANY)
    return pl.pallas_call(
        functools.partial(_dense_kernel, n_e=n_e, n_blocks=n_blocks),
        grid=(n_blocks + 2,),
        in_specs=[pl.BlockSpec((PEER_TM, d), lambda s: (tile(0)(s), 0)),
                  any_spec, any_spec, rspec, rspec, fspec, fspec],
        out_specs=pl.BlockSpec((d, PEER_TM), lambda s: (0, tile(2)(s))),
        out_shape=jax.ShapeDtypeStruct((d, t), F32),
        scratch_shapes=[pltpu.VMEM((2, PEER_TE, d), BF16), pltpu.VMEM((2, d, PEER_TE), BF16),
                        pltpu.SemaphoreType.DMA((2,)), pltpu.SemaphoreType.DMA((2,)),
                        pltpu.VMEM((PEER_TE, PEER_TM), F32), pltpu.VMEM((PEER_TE, PEER_TM), F32),
                        pltpu.VMEM((PEER_TE, PEER_TM), BF16), pltpu.VMEM((PEER_TE, PEER_TM), BF16),
                        pltpu.VMEM((d, PEER_TM), F32), pltpu.VMEM((PEER_TM, d), BF16),
                        pltpu.VMEM((PEER_HEADS, k1_blk, PEER_TM), F32),
                        pltpu.VMEM((PEER_HEADS, k1_blk, PEER_TM), F32),
                        pltpu.VMEM((PEER_HEADS, N_KEYS, PEER_TM), F32),
                        pltpu.VMEM((PEER_HEADS, N_KEYS, PEER_TM), F32)],
        compiler_params=_params("arbitrary"),
        name="peer_dense",
    )(h2, u_b, vt_b, tau, e1, s2, e2)


def _final_kernel(pt_ref, x1_ref, g2_ref, lg_ref, lb_ref, o_ref):
    p = pt_ref[...].T
    o_ref[...] = _layer_norm(DN_ALPHA * x1_ref[...] + g2_ref[...] * p, lg_ref[...], lb_ref[...])


def _peer_final(pt, x1, mod4, lg, lb, seq):
    t, d = x1.shape
    tm = 512
    tpb = seq // tm
    return pl.pallas_call(
        _final_kernel,
        grid=(t // tm,),
        in_specs=[pl.BlockSpec((d, tm), lambda i: (0, i)),
                  pl.BlockSpec((tm, d), lambda i: (i, 0)),
                  pl.BlockSpec((None, None, 1, d), lambda i: (i // tpb, 5, 0, 0)),
                  pl.BlockSpec((1, d), lambda i: (0, 0)),
                  pl.BlockSpec((1, d), lambda i: (0, 0))],
        out_specs=pl.BlockSpec((tm, d), lambda i: (i, 0)),
        out_shape=jax.ShapeDtypeStruct((t, d), F32),
        compiler_params=_params("arbitrary"),
        name="peer_final",
    )(pt, x1, mod4, lg, lb)


def _half_swap(w):
    half = w.shape[-1] // 2
    return jnp.concatenate([w[..., half:], w[..., :half]], axis=-1)


def _in_proj_layout(w, pad_to):
    o = np.cumsum([0, A_WIDTH, A_WIDTH, A_WIDTH, Q_LORA, KV_LORA, QK_ROPE, pad_to, pad_to])
    qa, ka, va, cq, ckv, kr, ga, gb = [w[..., o[n]:o[n + 1]] for n in range(8)]
    z64 = jnp.zeros(kr.shape[:-1] + (LANES - QK_ROPE,), w.dtype)
    ztail = jnp.zeros(kr.shape[:-1] + (IN_TN - 2 * LANES,), w.dtype)
    return jnp.concatenate([ga, gb, qa, ka, va, cq, ckv, kr, z64, _half_swap(kr), z64, ztail], axis=-1)


def _mla_weight_layout(w_uq, w_ukv):
    r = w_uq.shape[0]
    uq = w_uq.reshape(r, B_HEADS, QK_NOPE + QK_ROPE)
    nope, rope = uq[..., :QK_NOPE], uq[..., QK_NOPE:]
    z64 = jnp.zeros((r, B_HEADS, B_HEAD_PAD - QK_NOPE - QK_ROPE), w_uq.dtype)
    wq1 = jnp.concatenate([nope, rope, z64], axis=-1).reshape(r, B_HEADS * B_HEAD_PAD)
    wq2 = jnp.concatenate([_half_swap(rope), z64], axis=-1).reshape(r, B_HEADS * LANES)
    ukv = w_ukv.reshape(w_ukv.shape[0], B_HEADS, QK_NOPE + V_HEAD)
    wk = ukv[..., :QK_NOPE].reshape(-1, B_HEADS * QK_NOPE)
    wv = ukv[..., QK_NOPE:].reshape(-1, B_WIDTH)
    return wq1.astype(BF16), wq2.astype(BF16), wk.astype(BF16), wv.astype(BF16)


def kernel(x, c, positions, w_ada, b_ada, w_in, b_in, rel_bias, q_norm_g, kv_norm_g, w_uq, w_ukv,
           w_pa, w_pb, w_o, ln1_g, ln1_b, peer_wq, peer_keys, peer_u, peer_v, ln2_g, ln2_b):
    bsz, seq, d = x.shape
    t = bsz * seq
    assert d == GB0 - GA0 and seq % 1024 == 0 and w_ada.shape[0] == 1
    x2 = x.reshape(t, d)
    row = lambda a: a.reshape(1, -1).astype(F32)

    mod4 = _ada_mod(c, w_ada[0], b_ada[0]).reshape(bsz, 6, 1, d)
    cos_t, sin_t = _rope_tables(positions)

    w2 = _in_proj_layout(w_in[0], d).astype(BF16)
    b2 = _in_proj_layout(b_in[0], d).reshape(1, Z_WIDTH).astype(F32)
    z = _in_proj(x2, mod4, w2, b2, row(q_norm_g[0]), row(kv_norm_g[0]), seq)

    wq1, wq2, wk, wv = _mla_weight_layout(w_uq[0], w_ukv[0])
    qb, kb, vb = _mla_up(z, cos_t, sin_t, wq1, wq2, wk, wv)
    yb = _mla_flash(qb, kb, vb, bsz, seq)
    ya = _band_attn(z, _band_bias(rel_bias[0]), bsz, seq)

    x1, h2 = _out_proj(ya, yb, z, x2, mod4, w_pa[0].astype(BF16), w_pb[0].astype(BF16),
                       w_o[0].astype(BF16), row(ln1_g[0]), row(ln1_b[0]), seq)

    wqt = peer_wq[0].astype(BF16).T
    keys_b = peer_keys[0].reshape(2 * PEER_HEADS, N_KEYS, PEER_HALF).astype(BF16)
    tau, e1, s2, e2 = _peer_topk(h2, wqt, keys_b)
    pt = _peer_dense(h2, peer_u[0].astype(BF16), peer_v[0].astype(BF16).T, tau, e1, s2, e2)
    out = _peer_final(pt, x1, mod4, row(ln2_g[0]), row(ln2_b[0]), seq)
    return out.reshape(bsz, seq, d)
```

```python
import functools
import math

import jax
import jax.numpy as jnp
import numpy as np
from jax import lax
from jax.experimental import pallas as pl
from jax.experimental.pallas import tpu as pltpu

F32 = jnp.float32
BF16 = jnp.bfloat16

CHUNK = 64
A_HEADS = 8
A_HEAD_DIM = 128
A_LEFT_CHUNKS = 8
REL_CLIP_LEFT = 256
REL_CLIP_RIGHT = CHUNK - 1
B_HEADS = 8
Q_LORA = 512
KV_LORA = 512
QK_NOPE = 128
QK_ROPE = 64
V_HEAD = 128
ROPE_THETA = 10000.0
N_KEYS = 128
PEER_HEADS = 8
PEER_TOPK = 16
PEER_HALF = 128
DEPTH = 1
DN_ALPHA = (2 * DEPTH) ** 0.25
LN_EPS = 1e-5
RMS_EPS = 1e-6
NEG = -1e30

A_WIDTH = A_HEADS * A_HEAD_DIM
B_WIDTH = B_HEADS * V_HEAD
LOG2E = math.log2(math.e)
A_QSCALE = A_HEAD_DIM ** -0.5 * LOG2E
B_QSCALE = (QK_NOPE + QK_ROPE) ** -0.5 * LOG2E
SQRT_HALF = math.sqrt(0.5)

LANES = 128
SUBLANES_BF16 = 16
VMEM_LIMIT_BYTES = 56 * 1024 * 1024

IN_TN = 512
GA0 = 0
GB0 = 2048
QA0 = 4096
KA0 = 5120
VA0 = 6144
CQ0 = 7168
CKV0 = 7680
KR0 = 8192
Z_WIDTH = 8704
B_HEAD_PAD = 256


def _nt_dot(a, b):
    return lax.dot_general(a, b, (((1,), (1,)), ((), ())), preferred_element_type=F32)


def _params(*sem, flags=None, vmem=VMEM_LIMIT_BYTES):
    return pltpu.CompilerParams(dimension_semantics=sem, vmem_limit_bytes=vmem, flags=flags)


def _layer_norm(r, g, b):
    mu = jnp.mean(r, axis=-1, keepdims=True)
    d = r - mu
    var = jnp.mean(d * d, axis=-1, keepdims=True)
    return d * lax.rsqrt(var + LN_EPS) * g + b


def _ada_kernel(c_ref, w_ref, b_ref, o_ref):
    c = c_ref[...]
    cond = (c * jax.nn.sigmoid(c)).astype(BF16)
    o_ref[...] = jnp.dot(cond, w_ref[...].astype(BF16), preferred_element_type=F32) + b_ref[...]


def _ada_mod(c, w_ada, b_ada):
    bsz, d = c.shape
    n = w_ada.shape[1]
    tn = 1024
    return pl.pallas_call(
        _ada_kernel,
        grid=(n // tn,),
        in_specs=[pl.BlockSpec((bsz, d), lambda j: (0, 0)),
                  pl.BlockSpec((d, tn), lambda j: (0, j)),
                  pl.BlockSpec((1, tn), lambda j: (0, j))],
        out_specs=pl.BlockSpec((bsz, tn), lambda j: (0, j)),
        out_shape=jax.ShapeDtypeStruct((bsz, n), F32),
        compiler_params=_params("arbitrary"),
        name="ada_mod",
    )(c, w_ada, b_ada.reshape(1, n))


def _rope_kernel(pos_ref, invf_ref, cos_ref, sin_ref):
    ang = pos_ref[...] * invf_ref[...]
    cos_ref[...] = jnp.cos(ang)
    sin_ref[...] = jnp.sin(ang)


def _rope_tables(positions):
    t = positions.size
    half = QK_ROPE // 2
    per_row = LANES // half
    rows = t // per_row
    inv_freq = ROPE_THETA ** (-jnp.arange(0, QK_ROPE, 2, dtype=F32) / QK_ROPE)
    pos_rep = jnp.repeat(positions.astype(F32).reshape(rows, per_row), half, axis=1)
    invf = jnp.tile(inv_freq, per_row).reshape(1, LANES)
    tr = min(rows, 2048)
    cos, sin = pl.pallas_call(
        _rope_kernel,
        grid=(rows // tr,),
        in_specs=[pl.BlockSpec((tr, LANES), lambda i: (i, 0)),
                  pl.BlockSpec((1, LANES), lambda i: (0, 0))],
        out_specs=[pl.BlockSpec((tr, LANES), lambda i: (i, 0))] * 2,
        out_shape=[jax.ShapeDtypeStruct((rows, LANES), F32)] * 2,
        compiler_params=_params("arbitrary"),
        name="rope_tab",
    )(pos_rep, invf)
    cos = cos.reshape(t, half)
    sin = sin.reshape(t, half)
    zeros = jnp.zeros((t, LANES - QK_ROPE), F32)
    return (jnp.concatenate([cos, cos, zeros], axis=1),
            jnp.concatenate([-sin, sin, zeros], axis=1))


def _inproj_kernel(x_ref, sc_ref, sh_ref, w_ref, b_ref, gq_ref, gkv_ref, z_ref, h_scr):
    j = pl.program_id(1)

    @pl.when(j == 0)
    def _():
        h_scr[...] = (x_ref[...] * (1.0 + sc_ref[...]) + sh_ref[...]).astype(BF16)

    acc = jnp.dot(h_scr[...], w_ref[...], preferred_element_type=F32) + b_ref[...]

    def rms(g):
        ms = jnp.mean(acc * acc, axis=-1, keepdims=True)
        return acc * lax.rsqrt(ms + RMS_EPS) * g

    @pl.when(j < QA0 // IN_TN)
    def _():
        z_ref[...] = jax.nn.sigmoid(acc).astype(BF16)

    @pl.when(jnp.logical_and(j >= QA0 // IN_TN, j < KA0 // IN_TN))
    def _():
        z_ref[...] = (acc * A_QSCALE).astype(BF16)

    @pl.when(jnp.logical_or(jnp.logical_and(j >= KA0 // IN_TN, j < CQ0 // IN_TN), j == KR0 // IN_TN))
    def _():
        z_ref[...] = acc.astype(BF16)

    @pl.when(j == CQ0 // IN_TN)
    def _():
        z_ref[...] = rms(gq_ref[...]).astype(BF16)

    @pl.when(j == CKV0 // IN_TN)
    def _():
        z_ref[...] = rms(gkv_ref[...]).astype(BF16)


def _in_proj(x2, mod4, w2, b2, gq, gkv, seq):
    t, d = x2.shape
    tm = min(1024, seq)
    tpb = seq // tm
    return pl.pallas_call(
        _inproj_kernel,
        grid=(t // tm, Z_WIDTH // IN_TN),
        in_specs=[pl.BlockSpec((tm, d), lambda i, j: (i, 0)),
                  pl.BlockSpec((None, None, 1, d), lambda i, j: (i // tpb, 1, 0, 0)),
                  pl.BlockSpec((None, None, 1, d), lambda i, j: (i // tpb, 0, 0, 0)),
                  pl.BlockSpec((None, d, IN_TN), lambda i, j: (j, 0, 0)),
                  pl.BlockSpec((1, IN_TN), lambda i, j: (0, j)),
                  pl.BlockSpec((1, Q_LORA), lambda i, j: (0, 0)),
                  pl.BlockSpec((1, KV_LORA), lambda i, j: (0, 0))],
        out_specs=pl.BlockSpec((tm, IN_TN), lambda i, j: (i, j)),
        out_shape=jax.ShapeDtypeStruct((t, Z_WIDTH), BF16),
        scratch_shapes=[pltpu.VMEM((tm, d), BF16)],
        compiler_params=_params("arbitrary", "arbitrary"),
        name="in_proj",
    )(x2, mod4, mod4, w2, b2, gq, gkv)


FLASH_HEADS = 2


def _mla_up_kernel(cq_ref, ckv_ref, kra_ref, krb_ref, cos_ref, sin_ref,
                   wq1_ref, wq2_ref, wk_ref, wv_ref, q_ref, k_ref, v_ref):
    cq = cq_ref[...]
    ckv = ckv_ref[...]
    cos = cos_ref[...]
    sin = sin_ref[...]
    qa = jnp.dot(cq, wq1_ref[...], preferred_element_type=F32)
    qb = jnp.dot(cq, wq2_ref[...], preferred_element_type=F32)
    kn = jnp.dot(ckv, wk_ref[...], preferred_element_type=F32)
    vv = jnp.dot(ckv, wv_ref[...], preferred_element_type=F32).astype(BF16)
    k_rope = (kra_ref[...].astype(F32) * cos + krb_ref[...].astype(F32) * sin).astype(BF16)
    for h in range(B_HEADS):
        c0 = h * B_HEAD_PAD
        n0 = h * QK_NOPE
        grp = h // FLASH_HEADS
        o0 = (h % FLASH_HEADS) * B_HEAD_PAD
        v0 = (h % FLASH_HEADS) * V_HEAD
        q_ref[grp, :, o0:o0 + QK_NOPE] = (qa[:, c0:c0 + QK_NOPE] * B_QSCALE).astype(BF16)
        q_rope = qa[:, c0 + QK_NOPE:c0 + B_HEAD_PAD] * cos + qb[:, n0:n0 + QK_NOPE] * sin
        q_ref[grp, :, o0 + QK_NOPE:o0 + B_HEAD_PAD] = (q_rope * B_QSCALE).astype(BF16)
        k_ref[grp, :, o0:o0 + QK_NOPE] = kn[:, n0:n0 + QK_NOPE].astype(BF16)
        k_ref[grp, :, o0 + QK_NOPE:o0 + B_HEAD_PAD] = k_rope
        v_ref[grp, :, v0:v0 + V_HEAD] = vv[:, n0:n0 + V_HEAD]


def _mla_up(z, cos_t, sin_t, wq1, wq2, wk, wv):
    t = z.shape[0]
    tm = 512
    groups = B_HEADS // FLASH_HEADS
    const = lambda i: (0, 0)
    return pl.pallas_call(
        _mla_up_kernel,
        grid=(t // tm,),
        in_specs=[pl.BlockSpec((tm, Q_LORA), lambda i: (i, CQ0 // Q_LORA)),
                  pl.BlockSpec((tm, KV_LORA), lambda i: (i, CKV0 // KV_LORA)),
                  pl.BlockSpec((tm, LANES), lambda i: (i, KR0 // LANES)),
                  pl.BlockSpec((tm, LANES), lambda i: (i, KR0 // LANES + 1)),
                  pl.BlockSpec((tm, LANES), lambda i: (i, 0)),
                  pl.BlockSpec((tm, LANES), lambda i: (i, 0)),
                  pl.BlockSpec(wq1.shape, const),
                  pl.BlockSpec(wq2.shape, const),
                  pl.BlockSpec(wk.shape, const),
                  pl.BlockSpec(wv.shape, const)],
        out_specs=[pl.BlockSpec((groups, tm, FLASH_HEADS * B_HEAD_PAD), lambda i: (0, i, 0)),
                   pl.BlockSpec((groups, tm, FLASH_HEADS * B_HEAD_PAD), lambda i: (0, i, 0)),
                   pl.BlockSpec((groups, tm, FLASH_HEADS * V_HEAD), lambda i: (0, i, 0))],
        out_shape=[jax.ShapeDtypeStruct((groups, t, FLASH_HEADS * B_HEAD_PAD), BF16),
                   jax.ShapeDtypeStruct((groups, t, FLASH_HEADS * B_HEAD_PAD), BF16),
                   jax.ShapeDtypeStruct((groups, t, FLASH_HEADS * V_HEAD), BF16)],
        compiler_params=_params("arbitrary"),
        name="mla_up",
    )(z, z, z, z, cos_t, sin_t, wq1, wq2, wk, wv)


def _flash_kernel(qi_ref, kj_ref, q_ref, k_ref, v_ref, o_ref, q_s, k_s, v_s, m_scr, l_scr, acc_scr):
    p = pl.program_id(2)
    i = qi_ref[p]
    j = kj_ref[p]
    tq = q_ref.shape[0]
    tk = k_ref.shape[0]

    @pl.when(j == 0)
    def _():
        m_scr[...] = jnp.full(m_scr.shape, -jnp.inf, F32)
        l_scr[...] = jnp.zeros(l_scr.shape, F32)
        acc_scr[...] = jnp.zeros(acc_scr.shape, F32)

    def step(diagonal):
        q_s[...] = q_ref[...]
        k_s[...] = k_ref[...]
        v_s[...] = v_ref[...]
        for hh in range(FLASH_HEADS):
            qk = slice(hh * B_HEAD_PAD, (hh + 1) * B_HEAD_PAD)
            vv = slice(hh * V_HEAD, (hh + 1) * V_HEAD)
            s = _nt_dot(q_s[:, qk], k_s[:, qk])
            if diagonal:
                qpos = lax.broadcasted_iota(jnp.int32, (tq, tk), 0)
                kpos = lax.broadcasted_iota(jnp.int32, (tq, tk), 1)
                s = jnp.where(kpos <= (qpos | (CHUNK - 1)), s, NEG)
            m_prev = m_scr[hh]
            m_new = jnp.maximum(m_prev, jnp.max(s, axis=1, keepdims=True))
            alpha = jnp.exp2(m_prev - m_new)
            pexp = jnp.exp2(s - m_new)
            l_new = alpha * l_scr[hh] + jnp.sum(pexp, axis=1, keepdims=True)
            acc = alpha * acc_scr[hh] + jnp.dot(pexp.astype(BF16), v_s[:, vv],
                                                preferred_element_type=F32)
            m_scr[hh] = m_new
            l_scr[hh] = l_new
            acc_scr[hh] = acc
            if diagonal:
                o_ref[:, vv] = (acc / l_new).astype(BF16)

    @pl.when(j < i)
    def _():
        step(False)

    @pl.when(j == i)
    def _():
        step(True)


def _mla_flash(q, k, v, bsz, seq):
    tq = min(1024, seq)
    nqb = seq // tq
    pairs = [(i, j) for i in range(nqb) for j in range(i + 1)]
    qi = jnp.asarray([p[0] for p in pairs], jnp.int32)
    kj = jnp.asarray([p[1] for p in pairs], jnp.int32)
    qk_w = FLASH_HEADS * B_HEAD_PAD
    v_w = FLASH_HEADS * V_HEAD
    grid_spec = pltpu.PrefetchScalarGridSpec(
        num_scalar_prefetch=2,
        grid=(bsz, B_HEADS // FLASH_HEADS, len(pairs)),
        in_specs=[pl.BlockSpec((None, tq, qk_w), lambda b, h, p, qi, kj: (h, b * nqb + qi[p], 0)),
                  pl.BlockSpec((None, tq, qk_w), lambda b, h, p, qi, kj: (h, b * nqb + kj[p], 0)),
                  pl.BlockSpec((None, tq, v_w), lambda b, h, p, qi, kj: (h, b * nqb + kj[p], 0))],
        out_specs=pl.BlockSpec((tq, v_w), lambda b, h, p, qi, kj: (b * nqb + qi[p], h)),
        scratch_shapes=[pltpu.VMEM((tq, qk_w), BF16), pltpu.VMEM((tq, qk_w), BF16),
                        pltpu.VMEM((tq, v_w), BF16),
                        pltpu.VMEM((FLASH_HEADS, tq, 1), F32), pltpu.VMEM((FLASH_HEADS, tq, 1), F32),
                        pltpu.VMEM((FLASH_HEADS, tq, V_HEAD), F32)],
    )
    return pl.pallas_call(
        _flash_kernel,
        grid_spec=grid_spec,
        out_shape=jax.ShapeDtypeStruct((bsz * seq, B_WIDTH), BF16),
        compiler_params=_params("arbitrary", "arbitrary", "arbitrary"),
        name="mla_flash",
    )(qi, kj, q, k, v)


BAND_TQ = A_LEFT_CHUNKS * CHUNK


BAND_HEADS = 2


def _band_kernel(q_ref, kp_ref, kc_ref, vp_ref, vc_ref, bias_ref, o_ref):
    i = pl.program_id(2)
    first = jnp.where(i == 0, NEG, 0.0).astype(F32)
    outs = []
    for hh in range(BAND_HEADS):
        cols = slice(hh * A_HEAD_DIM, (hh + 1) * A_HEAD_DIM)
        q = q_ref[:, cols]
        s_prev = _nt_dot(q, kp_ref[:, cols]) + bias_ref[hh, :, :BAND_TQ] + first
        s_cur = _nt_dot(q, kc_ref[:, cols]) + bias_ref[hh, :, BAND_TQ:]
        m = jnp.maximum(jnp.max(s_prev, axis=1, keepdims=True), jnp.max(s_cur, axis=1, keepdims=True))
        p_prev = jnp.exp2(s_prev - m)
        p_cur = jnp.exp2(s_cur - m)
        l = jnp.sum(p_prev, axis=1, keepdims=True) + jnp.sum(p_cur, axis=1, keepdims=True)
        o = (jnp.dot(p_prev.astype(BF16), vp_ref[:, cols], preferred_element_type=F32)
             + jnp.dot(p_cur.astype(BF16), vc_ref[:, cols], preferred_element_type=F32))
        outs.append((o / l).astype(BF16))
    for hh in range(BAND_HEADS):
        o_ref[:, hh * A_HEAD_DIM:(hh + 1) * A_HEAD_DIM] = outs[hh]


def _bias_kernel(base_ref, o_ref):
    rows = jnp.broadcast_to(base_ref[...], (BAND_TQ, base_ref.shape[-1]))
    toep = pltpu.roll(rows, 0, 1, stride=1, stride_axis=0)[:, :2 * BAND_TQ]
    qc = lax.broadcasted_iota(jnp.int32, toep.shape, 0) // CHUNK + A_LEFT_CHUNKS
    kc = lax.broadcasted_iota(jnp.int32, toep.shape, 1) // CHUNK
    valid = jnp.logical_and(kc <= qc, kc >= qc - A_LEFT_CHUNKS)
    o_ref[...] = jnp.where(valid, toep * LOG2E, NEG)


def _band_bias(rel_bias):
    period = 3 * BAND_TQ
    heads, n_rel = rel_bias.shape
    rep = lambda col, n: jnp.broadcast_to(rel_bias[:, col:col + 1], (heads, n))
    n_hi = BAND_TQ - REL_CLIP_LEFT
    base = jnp.concatenate([rep(n_rel - 1, n_hi), rel_bias[:, ::-1],
                            rep(0, 2 * BAND_TQ + 1 - n_hi - n_rel), rep(n_rel - 1, BAND_TQ - 1)],
                           axis=1).astype(F32).reshape(heads, 1, period)
    return pl.pallas_call(
        _bias_kernel,
        grid=(heads,),
        in_specs=[pl.BlockSpec((None, 1, period), lambda h: (h, 0, 0))],
        out_specs=pl.BlockSpec((None, BAND_TQ, 2 * BAND_TQ), lambda h: (h, 0, 0)),
        out_shape=jax.ShapeDtypeStruct((heads, BAND_TQ, 2 * BAND_TQ), F32),
        compiler_params=_params("arbitrary"),
        name="band_bias",
    )(base)


def _band_attn(z, bias, bsz, seq):
    nq = seq // BAND_TQ
    width = BAND_HEADS * A_HEAD_DIM
    qb, kb, vb = QA0 // width, KA0 // width, VA0 // width
    cur = lambda c0: (lambda h, b, i: (b * nq + i, c0 + h))
    prev = lambda c0: (lambda h, b, i: (b * nq + jnp.maximum(i - 1, 0), c0 + h))
    blk = (BAND_TQ, width)
    return pl.pallas_call(
        _band_kernel,
        grid=(A_HEADS // BAND_HEADS, bsz, nq),
        in_specs=[pl.BlockSpec(blk, cur(qb)),
                  pl.BlockSpec(blk, prev(kb)), pl.BlockSpec(blk, cur(kb)),
                  pl.BlockSpec(blk, prev(vb)), pl.BlockSpec(blk, cur(vb)),
                  pl.BlockSpec((BAND_HEADS, BAND_TQ, 2 * BAND_TQ), lambda h, b, i: (h, 0, 0))],
        out_specs=pl.BlockSpec(blk, lambda h, b, i: (b * nq + i, h)),
        out_shape=jax.ShapeDtypeStruct((bsz * seq, A_WIDTH), BF16),
        compiler_params=_params("arbitrary", "arbitrary", "arbitrary"),
        name="band_attn",
    )(z, z, z, z, z, bias)


def _outproj_kernel(ya_ref, yb_ref, ga_ref, gb_ref, x_ref, g1_ref, sc2_ref, sh2_ref,
                    wpa_ref, wpb_ref, wo_ref, lg_ref, lb_ref, x1_ref, h2_ref):
    ta = jnp.dot(ya_ref[...], wpa_ref[...], preferred_element_type=F32)
    tb = jnp.dot(yb_ref[...], wpb_ref[...], preferred_element_type=F32)
    u = ga_ref[...].astype(F32) * ta + gb_ref[...].astype(F32) * tb
    y = jnp.dot(u.astype(BF16), wo_ref[...], preferred_element_type=F32)
    x1 = _layer_norm(DN_ALPHA * x_ref[...] + g1_ref[...] * y, lg_ref[...], lb_ref[...])
    x1_ref[...] = x1
    h2_ref[...] = (x1 * (1.0 + sc2_ref[...]) + sh2_ref[...]).astype(BF16)


def _out_proj(ya, yb, z, x2, mod4, wpa, wpb, wo, lg, lb, seq):
    t, d = x2.shape
    tm = 256
    tpb = seq // tm
    const = lambda i: (0, 0)
    modspec = lambda k: pl.BlockSpec((None, None, 1, d), lambda i: (i // tpb, k, 0, 0))
    return pl.pallas_call(
        _outproj_kernel,
        grid=(t // tm,),
        in_specs=[pl.BlockSpec((tm, A_WIDTH), lambda i: (i, 0)),
                  pl.BlockSpec((tm, B_WIDTH), lambda i: (i, 0)),
                  pl.BlockSpec((tm, d), lambda i: (i, GA0 // d)),
                  pl.BlockSpec((tm, d), lambda i: (i, GB0 // d)),
                  pl.BlockSpec((tm, d), lambda i: (i, 0)),
                  modspec(2), modspec(4), modspec(3),
                  pl.BlockSpec(wpa.shape, const), pl.BlockSpec(wpb.shape, const),
                  pl.BlockSpec(wo.shape, const),
                  pl.BlockSpec((1, d), const), pl.BlockSpec((1, d), const)],
        out_specs=[pl.BlockSpec((tm, d), lambda i: (i, 0)),
                   pl.BlockSpec((tm, d), lambda i: (i, 0))],
        out_shape=[jax.ShapeDtypeStruct((t, d), F32), jax.ShapeDtypeStruct((t, d), BF16)],
        compiler_params=_params("arbitrary"),
        name="out_proj",
    )(ya, yb, z, z, x2, mod4, mod4, mod4, wpa, wpb, wo, lg, lb)


TOPV = PEER_TOPK + 1
TOPV_ROWS = 24
SUBLANES_F32 = 8


def _sort_network(n):
    def merge(lo, hi, r):
        step = r * 2
        if step < hi - lo:
            yield from merge(lo, hi, step)
            yield from merge(lo + r, hi, step)
            yield from [(i, i + r) for i in range(lo + r, hi - r, step)]
        else:
            yield (lo, lo + r)

    def sort(lo, hi):
        if hi - lo >= 1:
            mid = lo + (hi - lo) // 2
            yield from sort(lo, mid)
            yield from sort(mid + 1, hi)
            yield from merge(lo, hi, 1)

    return list(sort(0, n - 1))


def _pop_top(lists, singles, out_scr):
    depth = len(lists)
    for r in range(TOPV):
        head = lists[0]
        for x in singles:
            head = jnp.maximum(head, x)
        m = jnp.max(head, axis=0, keepdims=True)
        out_scr[r:r + 1, :] = m
        remaining = TOPV - 1 - r
        if remaining == 0:
            break
        hit = lists[0] == m
        for k in range(min(depth, remaining)):
            nxt = lists[k + 1] if k + 1 < depth else -jnp.inf
            lists[k] = jnp.where(hit, nxt, lists[k])
        singles = [jnp.where(x == m, -jnp.inf, x) for x in singles]


def _top_vals(s, out_scr):
    n = s.shape[0] // SUBLANES_F32
    lists = [s[k * SUBLANES_F32:(k + 1) * SUBLANES_F32, :] for k in range(n)]
    for a, b in _sort_network(n):
        lists[a], lists[b] = jnp.maximum(lists[a], lists[b]), jnp.minimum(lists[a], lists[b])
    _pop_top(lists, [], out_scr)


def _topk_kernel(h2_ref, wqt_ref, keys_ref, tau_ref, e1_ref, s2_ref, e2_ref,
                 st_scr, tv1_scr, tv2_scr, fv_scr):
    pqt = _nt_dot(wqt_ref[...], h2_ref[...]).astype(BF16)
    for hp in range(2 * PEER_HEADS):
        r0 = hp * PEER_HALF
        st_scr[r0:r0 + N_KEYS, :] = jnp.dot(keys_ref[hp], pqt[r0:r0 + PEER_HALF, :],
                                            preferred_element_type=F32)
    pad = jnp.full((TOPV_ROWS - PEER_TOPK, h2_ref.shape[0]), -jnp.inf, F32)
    tv1_scr[PEER_TOPK:, :] = pad
    tv2_scr[PEER_TOPK:, :] = pad

    def head(h, carry):
        r1 = pl.multiple_of(h * 2 * N_KEYS, 2 * N_KEYS)
        s1 = st_scr[pl.ds(r1, N_KEYS), :]
        s2 = st_scr[pl.ds(r1 + N_KEYS, N_KEYS), :]
        _top_vals(s1, tv1_scr)
        _top_vals(s2, tv2_scr)
        tv1_lo = tv1_scr[0:SUBLANES_F32, :]
        lists = [tv1_lo + tv2_scr[j:j + 1, :] for j in range(PEER_TOPK)]
        singles = [tv1_scr[SUBLANES_F32:PEER_TOPK, :] + tv2_scr[0:1, :],
                   tv1_scr[PEER_TOPK:, :] + tv2_scr[0:1, :],
                   tv2_scr[PEER_TOPK:, :] + tv1_scr[0:1, :]]
        _pop_top(lists, singles, fv_scr)
        top = fv_scr[0:1, :]
        kept = fv_scr[0:PEER_TOPK, :]
        theta = 0.5 * (fv_scr[PEER_TOPK - 1:PEER_TOPK, :] + fv_scr[PEER_TOPK:PEER_TOPK + 1, :])
        inv_z = 1.0 / jnp.sum(jnp.exp(kept - top), axis=0, keepdims=True)
        tau_ref[h] = theta - s1
        e1_ref[h] = jnp.exp(s1 - tv1_scr[0:1, :]) * inv_z
        s2_ref[h] = s2
        e2_ref[h] = jnp.exp(s2 - tv2_scr[0:1, :])
        return carry

    lax.fori_loop(0, PEER_HEADS, head, 0)


def _peer_topk(h2, wqt, keys_b):
    t, d = h2.shape
    tm = 512
    rows = PEER_HEADS * 2 * N_KEYS
    out = jax.ShapeDtypeStruct((PEER_HEADS, N_KEYS, t), F32)
    ospec = pl.BlockSpec((PEER_HEADS, N_KEYS, tm), lambda i: (0, 0, i))
    return pl.pallas_call(
        _topk_kernel,
        grid=(t // tm,),
        in_specs=[pl.BlockSpec((tm, d), lambda i: (i, 0)),
                  pl.BlockSpec(wqt.shape, lambda i: (0, 0)),
                  pl.BlockSpec(keys_b.shape, lambda i: (0, 0, 0))],
        out_specs=[ospec] * 4,
        out_shape=[out] * 4,
        scratch_shapes=[pltpu.VMEM((rows, tm), F32),
                        pltpu.VMEM((TOPV_ROWS, tm), F32),
                        pltpu.VMEM((TOPV_ROWS, tm), F32),
                        pltpu.VMEM((TOPV_ROWS, tm), F32)],
        compiler_params=_params("arbitrary"),
        name="peer_topk",
    )(h2, wqt, keys_b)


PEER_TE = 1024
PEER_TM = 512
GATE_K1_TILE = 2


def _dense_stages(u_blk, vt_blk, h2_s, tau_s, e1_s, s2_s, e2_s, acc_s, at_new, at_old, wt_new, wt_old):
    acc_s[...] += jnp.dot(vt_blk[...], wt_old[...], preferred_element_type=F32)
    at_new[...] = _nt_dot(u_blk[...], h2_s[...])
    rows = SUBLANES_BF16
    for lb in range(PEER_TM // LANES):
        lanes = slice(lb * LANES, (lb + 1) * LANES)
        for g in range(N_KEYS // rows):
            krows = slice(g * rows, (g + 1) * rows)
            for k0 in range(0, PEER_TE // N_KEYS, GATE_K1_TILE):
                gates = [jnp.zeros((rows, LANES), F32) for _ in range(GATE_K1_TILE)]
                for h in range(PEER_HEADS):
                    s2 = s2_s[h, krows, lanes]
                    e2 = e2_s[h, krows, lanes]
                    for j in range(GATE_K1_TILE):
                        tau = tau_s[h, k0 + j:k0 + j + 1, lanes]
                        e1 = e1_s[h, k0 + j:k0 + j + 1, lanes]
                        gates[j] = gates[j] + jnp.where(s2 >= tau, e2, 0.0) * e1
                for j in range(GATE_K1_TILE):
                    r0 = (k0 + j) * N_KEYS + g * rows
                    a = at_old[r0:r0 + rows, lanes]
                    w = 0.5 * a * (1.0 + lax.erf(a * SQRT_HALF)) * gates[j]
                    wt_new[r0:r0 + rows, lanes] = w.astype(BF16)


def _dense_kernel(h2_ref, u_hbm, vt_hbm, tau_ref, e1_ref, s2_ref, e2_ref, pt_ref,
                  u_buf, vt_buf, u_sem, vt_sem, at0_scr, at1_scr, wt0_scr, wt1_scr,
                  acc_s, h2_s, tau_s, e1_s, s2_s, e2_s, *, n_e, n_blocks):
    s = pl.program_id(0)
    n_steps = n_blocks + 2
    blk = lambda step, lag: jnp.clip(step - lag, 0, n_blocks - 1)
    e_a = lax.rem(blk(s, 0), n_e)
    e_b = lax.rem(blk(s, 1), n_e)
    e_c = lax.rem(blk(s, 2), n_e)

    def u_copy(step, slot):
        rows = pl.ds(pl.multiple_of(lax.rem(blk(step, 0), n_e) * PEER_TE, PEER_TE), PEER_TE)
        return pltpu.make_async_copy(u_hbm.at[rows, :], u_buf.at[slot], u_sem.at[slot])

    def vt_copy(step, slot):
        return pltpu.make_async_copy(vt_hbm.at[lax.rem(blk(step, 2), n_e)], vt_buf.at[slot],
                                     vt_sem.at[slot])

    @pl.when(s == 0)
    def _():
        for scr in (at0_scr, at1_scr, wt0_scr, wt1_scr):
            scr[...] = jnp.zeros(scr.shape, scr.dtype)
        u_copy(s, 0).start()
        vt_copy(s, 0).start()

    @pl.when(e_a == 0)
    def _():
        h2_s[...] = h2_ref[...]

    @pl.when(e_b == 0)
    def _():
        s2_s[...] = s2_ref[...]
        e2_s[...] = e2_ref[...]

    @pl.when(e_c == 0)
    def _():
        acc_s[...] = jnp.zeros(acc_s.shape, F32)

    def step_body(slot, at_new, at_old, wt_new, wt_old):
        @pl.when(s + 1 < n_steps)
        def _():
            u_copy(s + 1, 1 - slot).start()
            vt_copy(s + 1, 1 - slot).start()

        u_copy(s, slot).wait()
        vt_copy(s, slot).wait()
        tau_s[...] = tau_ref[...]
        e1_s[...] = e1_ref[...]
        _dense_stages(u_buf.at[slot], vt_buf.at[slot], h2_s, tau_s, e1_s, s2_s, e2_s, acc_s,
                      at_new, at_old, wt_new, wt_old)

    @pl.when(lax.rem(s, 2) == 0)
    def _():
        step_body(0, at0_scr, at1_scr, wt1_scr, wt0_scr)

    @pl.when(lax.rem(s, 2) == 1)
    def _():
        step_body(1, at1_scr, at0_scr, wt0_scr, wt1_scr)

    @pl.when(jnp.logical_and(e_c == n_e - 1, s >= 2))
    def _():
        pt_ref[...] = acc_s[...]


def _peer_dense(h2, u_b, vt_b, tau, e1, s2, e2):
    t, d = h2.shape
    n_e = u_b.shape[0] // PEER_TE
    n_blocks = (t // PEER_TM) * n_e
    k1_blk = PEER_TE // N_KEYS
    blk = lambda lag: (lambda s: jnp.clip(s - lag, 0, n_blocks - 1))
    tile = lambda lag: (lambda s: blk(lag)(s) // n_e)
    eblk = lambda lag: (lambda s: lax.rem(blk(lag)(s), n_e))
    rspec = pl.BlockSpec((PEER_HEADS, k1_blk, PEER_TM), lambda s: (0, eblk(1)(s), tile(1)(s)))
    fspec = pl.BlockSpec((PEER_HEADS, N_KEYS, PEER_TM), lambda s: (0, 0, tile(1)(s)),
                         pipeline_mode=pl.Buffered(1))
    any_spec = pl.BlockSpec(memory_space=pl.ANY)
    return pl.pallas_call(
        functools.partial(_dense_kernel, n_e=n_e, n_blocks=n_blocks),
        grid=(n_blocks + 2,),
        in_specs=[pl.BlockSpec((PEER_TM, d), lambda s: (tile(0)(s), 0)),
                  any_spec, any_spec, rspec, rspec, fspec, fspec],
        out_specs=pl.BlockSpec((d, PEER_TM), lambda s: (0, tile(2)(s))),
        out_shape=jax.ShapeDtypeStruct((d, t), F32),
        scratch_shapes=[pltpu.VMEM((2, PEER_TE, d), BF16), pltpu.VMEM((2, d, PEER_TE), BF16),
                        pltpu.SemaphoreType.DMA((2,)), pltpu.SemaphoreType.DMA((2,)),
                        pltpu.VMEM((PEER_TE, PEER_TM), F32), pltpu.VMEM((PEER_TE, PEER_TM), F32),
                        pltpu.VMEM((PEER_TE, PEER_TM), BF16), pltpu.VMEM((PEER_TE, PEER_TM), BF16),
                        pltpu.VMEM((d, PEER_TM), F32), pltpu.VMEM((PEER_TM, d), BF16),
                        pltpu.VMEM((PEER_HEADS, k1_blk, PEER_TM), F32),
                        pltpu.VMEM((PEER_HEADS, k1_blk, PEER_TM), F32),
                        pltpu.VMEM((PEER_HEADS, N_KEYS, PEER_TM), F32),
                        pltpu.VMEM((PEER_HEADS, N_KEYS, PEER_TM), F32)],
        compiler_params=_params("arbitrary"),
        name="peer_dense",
    )(h2, u_b, vt_b, tau, e1, s2, e2)


def _final_kernel(pt_ref, x1_ref, g2_ref, lg_ref, lb_ref, o_ref):
    p = pt_ref[...].T
    o_ref[...] = _layer_norm(DN_ALPHA * x1_ref[...] + g2_ref[...] * p, lg_ref[...], lb_ref[...])


def _peer_final(pt, x1, mod4, lg, lb, seq):
    t, d = x1.shape
    tm = 512
    tpb = seq // tm
    return pl.pallas_call(
        _final_kernel,
        grid=(t // tm,),
        in_specs=[pl.BlockSpec((d, tm), lambda i: (0, i)),
                  pl.BlockSpec((tm, d), lambda i: (i, 0)),
                  pl.BlockSpec((None, None, 1, d), lambda i: (i // tpb, 5, 0, 0)),
                  pl.BlockSpec((1, d), lambda i: (0, 0)),
                  pl.BlockSpec((1, d), lambda i: (0, 0))],
        out_specs=pl.BlockSpec((tm, d), lambda i: (i, 0)),
        out_shape=jax.ShapeDtypeStruct((t, d), F32),
        compiler_params=_params("arbitrary"),
        name="peer_final",
    )(pt, x1, mod4, lg, lb)


def _half_swap(w):
    half = w.shape[-1] // 2
    return jnp.concatenate([w[..., half:], w[..., :half]], axis=-1)


def _in_proj_layout(w, pad_to):
    o = np.cumsum([0, A_WIDTH, A_WIDTH, A_WIDTH, Q_LORA, KV_LORA, QK_ROPE, pad_to, pad_to])
    qa, ka, va, cq, ckv, kr, ga, gb = [w[..., o[n]:o[n + 1]] for n in range(8)]
    z64 = jnp.zeros(kr.shape[:-1] + (LANES - QK_ROPE,), w.dtype)
    ztail = jnp.zeros(kr.shape[:-1] + (IN_TN - 2 * LANES,), w.dtype)
    return jnp.concatenate([ga, gb, qa, ka, va, cq, ckv, kr, z64, _half_swap(kr), z64, ztail], axis=-1)


def _mla_weight_layout(w_uq, w_ukv):
    r = w_uq.shape[0]
    uq = w_uq.reshape(r, B_HEADS, QK_NOPE + QK_ROPE)
    nope, rope = uq[..., :QK_NOPE], uq[..., QK_NOPE:]
    z64 = jnp.zeros((r, B_HEADS, B_HEAD_PAD - QK_NOPE - QK_ROPE), w_uq.dtype)
    wq1 = jnp.concatenate([nope, rope, z64], axis=-1).reshape(r, B_HEADS * B_HEAD_PAD)
    wq2 = jnp.concatenate([_half_swap(rope), z64], axis=-1).reshape(r, B_HEADS * LANES)
    ukv = w_ukv.reshape(w_ukv.shape[0], B_HEADS, QK_NOPE + V_HEAD)
    wk = ukv[..., :QK_NOPE].reshape(-1, B_HEADS * QK_NOPE)
    wv = ukv[..., QK_NOPE:].reshape(-1, B_WIDTH)
    return wq1.astype(BF16), wq2.astype(BF16), wk.astype(BF16), wv.astype(BF16)


def kernel(x, c, positions, w_ada, b_ada, w_in, b_in, rel_bias, q_norm_g, kv_norm_g, w_uq, w_ukv,
           w_pa, w_pb, w_o, ln1_g, ln1_b, peer_wq, peer_keys, peer_u, peer_v, ln2_g, ln2_b):
    bsz, seq, d = x.shape
    t = bsz * seq
    assert d == GB0 - GA0 and seq % 1024 == 0 and w_ada.shape[0] == 1
    x2 = x.reshape(t, d)
    row = lambda a: a.reshape(1, -1).astype(F32)

    mod4 = _ada_mod(c, w_ada[0], b_ada[0]).reshape(bsz, 6, 1, d)
    cos_t, sin_t = _rope_tables(positions)

    w2 = _in_proj_layout(w_in[0], d).astype(BF16).reshape(d, Z_WIDTH // IN_TN, IN_TN).transpose(1, 0, 2)
    b2 = _in_proj_layout(b_in[0], d).reshape(1, Z_WIDTH).astype(F32)
    z = _in_proj(x2, mod4, w2, b2, row(q_norm_g[0]), row(kv_norm_g[0]), seq)

    wq1, wq2, wk, wv = _mla_weight_layout(w_uq[0], w_ukv[0])
    qb, kb, vb = _mla_up(z, cos_t, sin_t, wq1, wq2, wk, wv)
    yb = _mla_flash(qb, kb, vb, bsz, seq)
    ya = _band_attn(z, _band_bias(rel_bias[0]), bsz, seq)

    x1, h2 = _out_proj(ya, yb, z, x2, mod4, w_pa[0].astype(BF16), w_pb[0].astype(BF16),
                       w_o[0].astype(BF16), row(ln1_g[0]), row(ln1_b[0]), seq)

    wqt = peer_wq[0].astype(BF16).T
    keys_b = peer_keys[0].reshape(2 * PEER_HEADS, N_KEYS, PEER_HALF).astype(BF16)
    tau, e1, s2, e2 = _peer_topk(h2, wqt, keys_b)
    vt_b = peer_v[0].astype(BF16).reshape(-1, PEER_TE, d).transpose(0, 2, 1)
    pt = _peer_dense(h2, peer_u[0].astype(BF16), vt_b, tau, e1, s2, e2)
    out = _peer_final(pt, x1, mod4, row(ln2_g[0]), row(ln2_b[0]), seq)
    return out.reshape(bsz, seq, d)
```

```python
import functools
import math

import jax
import jax.numpy as jnp
import numpy as np
from jax import lax
from jax.experimental import pallas as pl
from jax.experimental.pallas import tpu as pltpu

F32 = jnp.float32
BF16 = jnp.bfloat16

CHUNK = 64
A_HEADS = 8
A_HEAD_DIM = 128
A_LEFT_CHUNKS = 8
REL_CLIP_LEFT = 256
REL_CLIP_RIGHT = CHUNK - 1
B_HEADS = 8
Q_LORA = 512
KV_LORA = 512
QK_NOPE = 128
QK_ROPE = 64
V_HEAD = 128
ROPE_THETA = 10000.0
N_KEYS = 128
PEER_HEADS = 8
PEER_TOPK = 16
PEER_HALF = 128
DEPTH = 1
DN_ALPHA = (2 * DEPTH) ** 0.25
LN_EPS = 1e-5
RMS_EPS = 1e-6
NEG = -1e30

A_WIDTH = A_HEADS * A_HEAD_DIM
B_WIDTH = B_HEADS * V_HEAD
LOG2E = math.log2(math.e)
A_QSCALE = A_HEAD_DIM ** -0.5 * LOG2E
B_QSCALE = (QK_NOPE + QK_ROPE) ** -0.5 * LOG2E
SQRT_HALF = math.sqrt(0.5)

LANES = 128
SUBLANES_BF16 = 16
VMEM_LIMIT_BYTES = 56 * 1024 * 1024

IN_TN = 512
GA0 = 0
GB0 = 2048
QA0 = 4096
KA0 = 5120
VA0 = 6144
CQ0 = 7168
CKV0 = 7680
KR0 = 8192
Z_WIDTH = 8704
B_HEAD_PAD = 256


def _nt_dot(a, b):
    return lax.dot_general(a, b, (((1,), (1,)), ((), ())), preferred_element_type=F32)


def _params(*sem, flags=None, vmem=VMEM_LIMIT_BYTES):
    return pltpu.CompilerParams(dimension_semantics=sem, vmem_limit_bytes=vmem, flags=flags)


def _layer_norm(r, g, b):
    mu = jnp.mean(r, axis=-1, keepdims=True)
    d = r - mu
    var = jnp.mean(d * d, axis=-1, keepdims=True)
    return d * lax.rsqrt(var + LN_EPS) * g + b


def _ada_kernel(c_ref, w_ref, b_ref, o_ref):
    c = c_ref[...]
    cond = (c * jax.nn.sigmoid(c)).astype(BF16)
    o_ref[...] = jnp.dot(cond, w_ref[...].astype(BF16), preferred_element_type=F32) + b_ref[...]


def _ada_mod(c, w_ada, b_ada):
    bsz, d = c.shape
    n = w_ada.shape[1]
    tn = 1024
    return pl.pallas_call(
        _ada_kernel,
        grid=(n // tn,),
        in_specs=[pl.BlockSpec((bsz, d), lambda j: (0, 0)),
                  pl.BlockSpec((d, tn), lambda j: (0, j)),
                  pl.BlockSpec((1, tn), lambda j: (0, j))],
        out_specs=pl.BlockSpec((bsz, tn), lambda j: (0, j)),
        out_shape=jax.ShapeDtypeStruct((bsz, n), F32),
        compiler_params=_params("arbitrary"),
        name="ada_mod",
    )(c, w_ada, b_ada.reshape(1, n))


def _rope_kernel(pos_ref, invf_ref, cos_ref, sin_ref):
    ang = pos_ref[...] * invf_ref[...]
    cos_ref[...] = jnp.cos(ang)
    sin_ref[...] = jnp.sin(ang)


def _rope_tables(positions):
    t = positions.size
    half = QK_ROPE // 2
    per_row = LANES // half
    rows = t // per_row
    inv_freq = ROPE_THETA ** (-jnp.arange(0, QK_ROPE, 2, dtype=F32) / QK_ROPE)
    pos_rep = jnp.repeat(positions.astype(F32).reshape(rows, per_row), half, axis=1)
    invf = jnp.tile(inv_freq, per_row).reshape(1, LANES)
    tr = min(rows, 2048)
    cos, sin = pl.pallas_call(
        _rope_kernel,
        grid=(rows // tr,),
        in_specs=[pl.BlockSpec((tr, LANES), lambda i: (i, 0)),
                  pl.BlockSpec((1, LANES), lambda i: (0, 0))],
        out_specs=[pl.BlockSpec((tr, LANES), lambda i: (i, 0))] * 2,
        out_shape=[jax.ShapeDtypeStruct((rows, LANES), F32)] * 2,
        compiler_params=_params("arbitrary"),
        name="rope_tab",
    )(pos_rep, invf)
    cos = cos.reshape(t, half)
    sin = sin.reshape(t, half)
    zeros = jnp.zeros((t, LANES - QK_ROPE), F32)
    return (jnp.concatenate([cos, cos, zeros], axis=1),
            jnp.concatenate([-sin, sin, zeros], axis=1))


def _inproj_kernel(x_ref, sc_ref, sh_ref, w_ref, b_ref, gq_ref, gkv_ref, z_ref, h_scr):
    j = pl.program_id(1)

    @pl.when(j == 0)
    def _():
        h_scr[...] = (x_ref[...] * (1.0 + sc_ref[...]) + sh_ref[...]).astype(BF16)

    acc = jnp.dot(h_scr[...], w_ref[...], preferred_element_type=F32) + b_ref[...]

    def rms(g):
        ms = jnp.mean(acc * acc, axis=-1, keepdims=True)
        return acc * lax.rsqrt(ms + RMS_EPS) * g

    @pl.when(j < QA0 // IN_TN)
    def _():
        z_ref[...] = jax.nn.sigmoid(acc).astype(BF16)

    @pl.when(jnp.logical_and(j >= QA0 // IN_TN, j < KA0 // IN_TN))
    def _():
        z_ref[...] = (acc * A_QSCALE).astype(BF16)

    @pl.when(jnp.logical_or(jnp.logical_and(j >= KA0 // IN_TN, j < CQ0 // IN_TN), j == KR0 // IN_TN))
    def _():
        z_ref[...] = acc.astype(BF16)

    @pl.when(j == CQ0 // IN_TN)
    def _():
        z_ref[...] = rms(gq_ref[...]).astype(BF16)

    @pl.when(j == CKV0 // IN_TN)
    def _():
        z_ref[...] = rms(gkv_ref[...]).astype(BF16)


def _in_proj(x2, mod4, w2, b2, gq, gkv, seq):
    t, d = x2.shape
    tm = min(1024, seq)
    tpb = seq // tm
    return pl.pallas_call(
        _inproj_kernel,
        grid=(t // tm, Z_WIDTH // IN_TN),
        in_specs=[pl.BlockSpec((tm, d), lambda i, j: (i, 0)),
                  pl.BlockSpec((None, None, 1, d), lambda i, j: (i // tpb, 1, 0, 0)),
                  pl.BlockSpec((None, None, 1, d), lambda i, j: (i // tpb, 0, 0, 0)),
                  pl.BlockSpec((None, d, IN_TN), lambda i, j: (j, 0, 0)),
                  pl.BlockSpec((1, IN_TN), lambda i, j: (0, j)),
                  pl.BlockSpec((1, Q_LORA), lambda i, j: (0, 0)),
                  pl.BlockSpec((1, KV_LORA), lambda i, j: (0, 0))],
        out_specs=pl.BlockSpec((tm, IN_TN), lambda i, j: (i, j)),
        out_shape=jax.ShapeDtypeStruct((t, Z_WIDTH), BF16),
        scratch_shapes=[pltpu.VMEM((tm, d), BF16)],
        compiler_params=_params("arbitrary", "arbitrary"),
        name="in_proj",
    )(x2, mod4, mod4, w2, b2, gq, gkv)


FLASH_HEADS = 2


def _mla_up_kernel(cq_ref, ckv_ref, kra_ref, krb_ref, cos_ref, sin_ref,
                   wq1_ref, wq2_ref, wk_ref, wv_ref, q_ref, k_ref, v_ref):
    cq = cq_ref[...]
    ckv = ckv_ref[...]
    cos = cos_ref[...]
    sin = sin_ref[...]
    qa = jnp.dot(cq, wq1_ref[...], preferred_element_type=F32)
    qb = jnp.dot(cq, wq2_ref[...], preferred_element_type=F32)
    kn = jnp.dot(ckv, wk_ref[...], preferred_element_type=F32)
    vv = jnp.dot(ckv, wv_ref[...], preferred_element_type=F32).astype(BF16)
    k_rope = (kra_ref[...].astype(F32) * cos + krb_ref[...].astype(F32) * sin).astype(BF16)
    for h in range(B_HEADS):
        c0 = h * B_HEAD_PAD
        n0 = h * QK_NOPE
        grp = h // FLASH_HEADS
        o0 = (h % FLASH_HEADS) * B_HEAD_PAD
        v0 = (h % FLASH_HEADS) * V_HEAD
        q_ref[grp, :, o0:o0 + QK_NOPE] = (qa[:, c0:c0 + QK_NOPE] * B_QSCALE).astype(BF16)
        q_rope = qa[:, c0 + QK_NOPE:c0 + B_HEAD_PAD] * cos + qb[:, n0:n0 + QK_NOPE] * sin
        q_ref[grp, :, o0 + QK_NOPE:o0 + B_HEAD_PAD] = (q_rope * B_QSCALE).astype(BF16)
        k_ref[grp, :, o0:o0 + QK_NOPE] = kn[:, n0:n0 + QK_NOPE].astype(BF16)
        k_ref[grp, :, o0 + QK_NOPE:o0 + B_HEAD_PAD] = k_rope
        v_ref[grp, :, v0:v0 + V_HEAD] = vv[:, n0:n0 + V_HEAD]


def _mla_up(z, cos_t, sin_t, wq1, wq2, wk, wv):
    t = z.shape[0]
    tm = 512
    groups = B_HEADS // FLASH_HEADS
    const = lambda i: (0, 0)
    return pl.pallas_call(
        _mla_up_kernel,
        grid=(t // tm,),
        in_specs=[pl.BlockSpec((tm, Q_LORA), lambda i: (i, CQ0 // Q_LORA)),
                  pl.BlockSpec((tm, KV_LORA), lambda i: (i, CKV0 // KV_LORA)),
                  pl.BlockSpec((tm, LANES), lambda i: (i, KR0 // LANES)),
                  pl.BlockSpec((tm, LANES), lambda i: (i, KR0 // LANES + 1)),
                  pl.BlockSpec((tm, LANES), lambda i: (i, 0)),
                  pl.BlockSpec((tm, LANES), lambda i: (i, 0)),
                  pl.BlockSpec(wq1.shape, const),
                  pl.BlockSpec(wq2.shape, const),
                  pl.BlockSpec(wk.shape, const),
                  pl.BlockSpec(wv.shape, const)],
        out_specs=[pl.BlockSpec((groups, tm, FLASH_HEADS * B_HEAD_PAD), lambda i: (0, i, 0)),
                   pl.BlockSpec((groups, tm, FLASH_HEADS * B_HEAD_PAD), lambda i: (0, i, 0)),
                   pl.BlockSpec((groups, tm, FLASH_HEADS * V_HEAD), lambda i: (0, i, 0))],
        out_shape=[jax.ShapeDtypeStruct((groups, t, FLASH_HEADS * B_HEAD_PAD), BF16),
                   jax.ShapeDtypeStruct((groups, t, FLASH_HEADS * B_HEAD_PAD), BF16),
                   jax.ShapeDtypeStruct((groups, t, FLASH_HEADS * V_HEAD), BF16)],
        compiler_params=_params("arbitrary"),
        name="mla_up",
    )(z, z, z, z, cos_t, sin_t, wq1, wq2, wk, wv)


def _flash_kernel(qi_ref, kj_ref, q_ref, k_ref, v_ref, o_ref, q_s, k_s, v_s, m_scr, l_scr, acc_scr):
    p = pl.program_id(2)
    i = qi_ref[p]
    j = kj_ref[p]
    tq = q_ref.shape[0]
    tk = k_ref.shape[0]

    @pl.when(j == 0)
    def _():
        m_scr[...] = jnp.full(m_scr.shape, -jnp.inf, F32)
        l_scr[...] = jnp.zeros(l_scr.shape, F32)
        acc_scr[...] = jnp.zeros(acc_scr.shape, F32)

    def step(diagonal):
        q_s[...] = q_ref[...]
        k_s[...] = k_ref[...]
        v_s[...] = v_ref[...]
        for hh in range(FLASH_HEADS):
            qk = slice(hh * B_HEAD_PAD, (hh + 1) * B_HEAD_PAD)
            vv = slice(hh * V_HEAD, (hh + 1) * V_HEAD)
            s = _nt_dot(q_s[:, qk], k_s[:, qk])
            if diagonal:
                qpos = lax.broadcasted_iota(jnp.int32, (tq, tk), 0)
                kpos = lax.broadcasted_iota(jnp.int32, (tq, tk), 1)
                s = jnp.where(kpos <= (qpos | (CHUNK - 1)), s, NEG)
            m_prev = m_scr[hh]
            m_new = jnp.maximum(m_prev, jnp.max(s, axis=1, keepdims=True))
            alpha = jnp.exp2(m_prev - m_new)
            pexp = jnp.exp2(s - m_new)
            l_new = alpha * l_scr[hh] + jnp.sum(pexp, axis=1, keepdims=True)
            acc = alpha * acc_scr[hh] + jnp.dot(pexp.astype(BF16), v_s[:, vv],
                                                preferred_element_type=F32)
            m_scr[hh] = m_new
            l_scr[hh] = l_new
            acc_scr[hh] = acc
            if diagonal:
                o_ref[:, vv] = (acc / l_new).astype(BF16)

    @pl.when(j < i)
    def _():
        step(False)

    @pl.when(j == i)
    def _():
        step(True)


def _mla_flash(q, k, v, bsz, seq):
    tq = min(1024, seq)
    nqb = seq // tq
    pairs = [(i, j) for i in range(nqb) for j in range(i + 1)]
    qi = jnp.asarray([p[0] for p in pairs], jnp.int32)
    kj = jnp.asarray([p[1] for p in pairs], jnp.int32)
    qk_w = FLASH_HEADS * B_HEAD_PAD
    v_w = FLASH_HEADS * V_HEAD
    grid_spec = pltpu.PrefetchScalarGridSpec(
        num_scalar_prefetch=2,
        grid=(bsz, B_HEADS // FLASH_HEADS, len(pairs)),
        in_specs=[pl.BlockSpec((None, tq, qk_w), lambda b, h, p, qi, kj: (h, b * nqb + qi[p], 0)),
                  pl.BlockSpec((None, tq, qk_w), lambda b, h, p, qi, kj: (h, b * nqb + kj[p], 0)),
                  pl.BlockSpec((None, tq, v_w), lambda b, h, p, qi, kj: (h, b * nqb + kj[p], 0))],
        out_specs=pl.BlockSpec((tq, v_w), lambda b, h, p, qi, kj: (b * nqb + qi[p], h)),
        scratch_shapes=[pltpu.VMEM((tq, qk_w), BF16), pltpu.VMEM((tq, qk_w), BF16),
                        pltpu.VMEM((tq, v_w), BF16),
                        pltpu.VMEM((FLASH_HEADS, tq, 1), F32), pltpu.VMEM((FLASH_HEADS, tq, 1), F32),
                        pltpu.VMEM((FLASH_HEADS, tq, V_HEAD), F32)],
    )
    return pl.pallas_call(
        _flash_kernel,
        grid_spec=grid_spec,
        out_shape=jax.ShapeDtypeStruct((bsz * seq, B_WIDTH), BF16),
        compiler_params=_params("arbitrary", "arbitrary", "arbitrary"),
        name="mla_flash",
    )(qi, kj, q, k, v)


BAND_TQ = A_LEFT_CHUNKS * CHUNK


BAND_HEADS = 2


def _band_kernel(q_ref, kp_ref, kc_ref, vp_ref, vc_ref, bias_ref, o_ref):
    i = pl.program_id(2)
    first = jnp.where(i == 0, NEG, 0.0).astype(F32)
    outs = []
    for hh in range(BAND_HEADS):
        cols = slice(hh * A_HEAD_DIM, (hh + 1) * A_HEAD_DIM)
        q = q_ref[:, cols]
        s_prev = _nt_dot(q, kp_ref[:, cols]) + bias_ref[hh, :, :BAND_TQ] + first
        s_cur = _nt_dot(q, kc_ref[:, cols]) + bias_ref[hh, :, BAND_TQ:]
        m = jnp.maximum(jnp.max(s_prev, axis=1, keepdims=True), jnp.max(s_cur, axis=1, keepdims=True))
        p_prev = jnp.exp2(s_prev - m)
        p_cur = jnp.exp2(s_cur - m)
        l = jnp.sum(p_prev, axis=1, keepdims=True) + jnp.sum(p_cur, axis=1, keepdims=True)
        o = (jnp.dot(p_prev.astype(BF16), vp_ref[:, cols], preferred_element_type=F32)
             + jnp.dot(p_cur.astype(BF16), vc_ref[:, cols], preferred_element_type=F32))
        outs.append((o / l).astype(BF16))
    for hh in range(BAND_HEADS):
        o_ref[:, hh * A_HEAD_DIM:(hh + 1) * A_HEAD_DIM] = outs[hh]


def _bias_kernel(base_ref, o_ref):
    rows = jnp.broadcast_to(base_ref[...], (BAND_TQ, base_ref.shape[-1]))
    toep = pltpu.roll(rows, 0, 1, stride=1, stride_axis=0)[:, :2 * BAND_TQ]
    qc = lax.broadcasted_iota(jnp.int32, toep.shape, 0) // CHUNK + A_LEFT_CHUNKS
    kc = lax.broadcasted_iota(jnp.int32, toep.shape, 1) // CHUNK
    valid = jnp.logical_and(kc <= qc, kc >= qc - A_LEFT_CHUNKS)
    o_ref[...] = jnp.where(valid, toep * LOG2E, NEG)


def _band_bias(rel_bias):
    period = 3 * BAND_TQ
    heads, n_rel = rel_bias.shape
    rep = lambda col, n: jnp.broadcast_to(rel_bias[:, col:col + 1], (heads, n))
    n_hi = BAND_TQ - REL_CLIP_LEFT
    base = jnp.concatenate([rep(n_rel - 1, n_hi), rel_bias[:, ::-1],
                            rep(0, 2 * BAND_TQ + 1 - n_hi - n_rel), rep(n_rel - 1, BAND_TQ - 1)],
                           axis=1).astype(F32).reshape(heads, 1, period)
    return pl.pallas_call(
        _bias_kernel,
        grid=(heads,),
        in_specs=[pl.BlockSpec((None, 1, period), lambda h: (h, 0, 0))],
        out_specs=pl.BlockSpec((None, BAND_TQ, 2 * BAND_TQ), lambda h: (h, 0, 0)),
        out_shape=jax.ShapeDtypeStruct((heads, BAND_TQ, 2 * BAND_TQ), F32),
        compiler_params=_params("arbitrary"),
        name="band_bias",
    )(base)


def _band_attn(z, bias, bsz, seq):
    nq = seq // BAND_TQ
    width = BAND_HEADS * A_HEAD_DIM
    qb, kb, vb = QA0 // width, KA0 // width, VA0 // width
    cur = lambda c0: (lambda h, b, i: (b * nq + i, c0 + h))
    prev = lambda c0: (lambda h, b, i: (b * nq + jnp.maximum(i - 1, 0), c0 + h))
    blk = (BAND_TQ, width)
    return pl.pallas_call(
        _band_kernel,
        grid=(A_HEADS // BAND_HEADS, bsz, nq),
        in_specs=[pl.BlockSpec(blk, cur(qb)),
                  pl.BlockSpec(blk, prev(kb)), pl.BlockSpec(blk, cur(kb)),
                  pl.BlockSpec(blk, prev(vb)), pl.BlockSpec(blk, cur(vb)),
                  pl.BlockSpec((BAND_HEADS, BAND_TQ, 2 * BAND_TQ), lambda h, b, i: (h, 0, 0))],
        out_specs=pl.BlockSpec(blk, lambda h, b, i: (b * nq + i, h)),
        out_shape=jax.ShapeDtypeStruct((bsz * seq, A_WIDTH), BF16),
        compiler_params=_params("arbitrary", "arbitrary", "arbitrary"),
        name="band_attn",
    )(z, z, z, z, z, bias)


def _outproj_kernel(ya_ref, yb_ref, ga_ref, gb_ref, x_ref, g1_ref, sc2_ref, sh2_ref,
                    wpa_ref, wpb_ref, wo_ref, lg_ref, lb_ref, x1_ref, h2_ref):
    ta = jnp.dot(ya_ref[...], wpa_ref[...], preferred_element_type=F32)
    tb = jnp.dot(yb_ref[...], wpb_ref[...], preferred_element_type=F32)
    u = ga_ref[...].astype(F32) * ta + gb_ref[...].astype(F32) * tb
    y = jnp.dot(u.astype(BF16), wo_ref[...], preferred_element_type=F32)
    x1 = _layer_norm(DN_ALPHA * x_ref[...] + g1_ref[...] * y, lg_ref[...], lb_ref[...])
    x1_ref[...] = x1
    h2_ref[...] = (x1 * (1.0 + sc2_ref[...]) + sh2_ref[...]).astype(BF16)


def _out_proj(ya, yb, z, x2, mod4, wpa, wpb, wo, lg, lb, seq):
    t, d = x2.shape
    tm = 256
    tpb = seq // tm
    const = lambda i: (0, 0)
    modspec = lambda k: pl.BlockSpec((None, None, 1, d), lambda i: (i // tpb, k, 0, 0))
    return pl.pallas_call(
        _outproj_kernel,
        grid=(t // tm,),
        in_specs=[pl.BlockSpec((tm, A_WIDTH), lambda i: (i, 0)),
                  pl.BlockSpec((tm, B_WIDTH), lambda i: (i, 0)),
                  pl.BlockSpec((tm, d), lambda i: (i, GA0 // d)),
                  pl.BlockSpec((tm, d), lambda i: (i, GB0 // d)),
                  pl.BlockSpec((tm, d), lambda i: (i, 0)),
                  modspec(2), modspec(4), modspec(3),
                  pl.BlockSpec(wpa.shape, const), pl.BlockSpec(wpb.shape, const),
                  pl.BlockSpec(wo.shape, const),
                  pl.BlockSpec((1, d), const), pl.BlockSpec((1, d), const)],
        out_specs=[pl.BlockSpec((tm, d), lambda i: (i, 0)),
                   pl.BlockSpec((tm, d), lambda i: (i, 0))],
        out_shape=[jax.ShapeDtypeStruct((t, d), F32), jax.ShapeDtypeStruct((t, d), BF16)],
        compiler_params=_params("arbitrary"),
        name="out_proj",
    )(ya, yb, z, z, x2, mod4, mod4, mod4, wpa, wpb, wo, lg, lb)


TOPV = PEER_TOPK + 1
TOPV_ROWS = 24
SUBLANES_F32 = 8


def _sort_network(n):
    def merge(lo, hi, r):
        step = r * 2
        if step < hi - lo:
            yield from merge(lo, hi, step)
            yield from merge(lo + r, hi, step)
            yield from [(i, i + r) for i in range(lo + r, hi - r, step)]
        else:
            yield (lo, lo + r)

    def sort(lo, hi):
        if hi - lo >= 1:
            mid = lo + (hi - lo) // 2
            yield from sort(lo, mid)
            yield from sort(mid + 1, hi)
            yield from merge(lo, hi, 1)

    return list(sort(0, n - 1))


def _pop_top(lists, singles, out_scr):
    depth = len(lists)
    for r in range(TOPV):
        head = lists[0]
        for x in singles:
            head = jnp.maximum(head, x)
        m = jnp.max(head, axis=0, keepdims=True)
        out_scr[r:r + 1, :] = m
        remaining = TOPV - 1 - r
        if remaining == 0:
            break
        hit = lists[0] == m
        for k in range(min(depth, remaining)):
            nxt = lists[k + 1] if k + 1 < depth else -jnp.inf
            lists[k] = jnp.where(hit, nxt, lists[k])
        singles = [jnp.where(x == m, -jnp.inf, x) for x in singles]


def _top_vals(s, out_scr):
    n = s.shape[0] // SUBLANES_F32
    lists = [s[k * SUBLANES_F32:(k + 1) * SUBLANES_F32, :] for k in range(n)]
    for a, b in _sort_network(n):
        lists[a], lists[b] = jnp.maximum(lists[a], lists[b]), jnp.minimum(lists[a], lists[b])
    _pop_top(lists, [], out_scr)


def _topk_kernel(h2_ref, wqt_ref, keys_ref, c1_ref, e1_ref, r2_ref, e2_ref,
                 st_scr, tv1_scr, tv2_scr, fv_scr):
    pqt = _nt_dot(wqt_ref[...], h2_ref[...]).astype(BF16)
    for hp in range(2 * PEER_HEADS):
        r0 = hp * PEER_HALF
        st_scr[r0:r0 + N_KEYS, :] = jnp.dot(keys_ref[hp], pqt[r0:r0 + PEER_HALF, :],
                                            preferred_element_type=F32)
    pad = jnp.full((TOPV_ROWS - PEER_TOPK, h2_ref.shape[0]), -jnp.inf, F32)
    tv1_scr[PEER_TOPK:, :] = pad
    tv2_scr[PEER_TOPK:, :] = pad

    def head(h, carry):
        r1 = pl.multiple_of(h * 2 * N_KEYS, 2 * N_KEYS)
        s1 = st_scr[pl.ds(r1, N_KEYS), :]
        s2 = st_scr[pl.ds(r1 + N_KEYS, N_KEYS), :]
        _top_vals(s1, tv1_scr)
        _top_vals(s2, tv2_scr)
        tv1_lo = tv1_scr[0:SUBLANES_F32, :]
        lists = [tv1_lo + tv2_scr[j:j + 1, :] for j in range(PEER_TOPK)]
        singles = [tv1_scr[SUBLANES_F32:PEER_TOPK, :] + tv2_scr[0:1, :],
                   tv1_scr[PEER_TOPK:, :] + tv2_scr[0:1, :],
                   tv2_scr[PEER_TOPK:, :] + tv1_scr[0:1, :]]
        _pop_top(lists, singles, fv_scr)
        top = fv_scr[0:1, :]
        kept = fv_scr[0:PEER_TOPK, :]
        theta = 0.5 * (fv_scr[PEER_TOPK - 1:PEER_TOPK, :] + fv_scr[PEER_TOPK:PEER_TOPK + 1, :])
        inv_z = 1.0 / jnp.sum(jnp.exp(kept - top), axis=0, keepdims=True)
        tau = theta - s1
        count1 = jnp.zeros_like(s1)
        rank2 = jnp.zeros_like(s2)
        for j in range(TOPV):
            tv2_j = tv2_scr[j:j + 1, :]
            count1 = jnp.where(tv2_j >= tau, float(j + 1), count1)
            rank2 = jnp.where(tv2_j > s2, float(j + 1), rank2)
        c1_ref[h] = count1
        e1_ref[h] = jnp.exp(s1 - tv1_scr[0:1, :]) * inv_z
        r2_ref[h] = rank2.astype(BF16)
        e2_ref[h] = jnp.exp(s2 - tv2_scr[0:1, :]).astype(BF16)
        return carry

    lax.fori_loop(0, PEER_HEADS, head, 0)


def _peer_topk(h2, wqt, keys_b):
    t, d = h2.shape
    tm = 512
    rows = PEER_HEADS * 2 * N_KEYS
    out = lambda dt: jax.ShapeDtypeStruct((PEER_HEADS, N_KEYS, t), dt)
    ospec = pl.BlockSpec((PEER_HEADS, N_KEYS, tm), lambda i: (0, 0, i))
    return pl.pallas_call(
        _topk_kernel,
        grid=(t // tm,),
        in_specs=[pl.BlockSpec((tm, d), lambda i: (i, 0)),
                  pl.BlockSpec(wqt.shape, lambda i: (0, 0)),
                  pl.BlockSpec(keys_b.shape, lambda i: (0, 0, 0))],
        out_specs=[ospec] * 4,
        out_shape=[out(F32), out(F32), out(BF16), out(BF16)],
        scratch_shapes=[pltpu.VMEM((rows, tm), F32),
                        pltpu.VMEM((TOPV_ROWS, tm), F32),
                        pltpu.VMEM((TOPV_ROWS, tm), F32),
                        pltpu.VMEM((TOPV_ROWS, tm), F32)],
        compiler_params=_params("arbitrary"),
        name="peer_topk",
    )(h2, wqt, keys_b)


PEER_TE = 1024
PEER_TM = 512
GATE_K1_TILE = 8


def _dense_stages(u_blk, vt_blk, h2_s, c1b_s, e1b_s, r2_s, e2_s, acc_s, at_new, at_old, wt_new, wt_old):
    acc_s[...] += jnp.dot(vt_blk[...], wt_old[...], preferred_element_type=F32)
    at_new[...] = _nt_dot(u_blk[...], h2_s[...])
    rows = SUBLANES_BF16
    k1_blk = PEER_TE // N_KEYS
    for lb in range(PEER_TM // LANES):
        lanes = slice(lb * LANES, (lb + 1) * LANES)
        for g in range(N_KEYS // rows):
            krows = slice(g * rows, (g + 1) * rows)
            for k0 in range(0, k1_blk, GATE_K1_TILE):
                gates = [jnp.zeros((rows, LANES), BF16) for _ in range(GATE_K1_TILE)]
                for h in range(PEER_HEADS):
                    r2 = r2_s[h, krows, lanes]
                    e2 = e2_s[h, krows, lanes]
                    for j in range(GATE_K1_TILE):
                        c1 = c1b_s[lb, h * k1_blk + k0 + j]
                        e1 = e1b_s[lb, h * k1_blk + k0 + j]
                        gates[j] = gates[j] + jnp.where(r2 < c1, e2, jnp.zeros_like(e2)) * e1
                tiles = []
                for j in range(GATE_K1_TILE):
                    r0 = (k0 + j) * N_KEYS + g * rows
                    a = at_old[r0:r0 + rows, lanes]
                    gelu = 0.5 * a * (1.0 + lax.erf(a * SQRT_HALF))
                    tiles.append((r0, gelu.astype(BF16) * gates[j]))
                for r0, w in tiles:
                    wt_new[r0:r0 + rows, lanes] = w


def _dense_kernel(h2_ref, u_hbm, vt_hbm, c1_ref, e1_ref, r2_ref, e2_ref, pt_ref,
                  u_buf, vt_buf, u_sem, vt_sem, at0_scr, at1_scr, wt0_scr, wt1_scr,
                  acc_s, h2_s, c1b_s, e1b_s, r2_s, e2_s, *, n_e, n_blocks):
    s = pl.program_id(0)
    n_steps = n_blocks + 2
    blk = lambda step, lag: jnp.clip(step - lag, 0, n_blocks - 1)
    e_a = lax.rem(blk(s, 0), n_e)
    e_b = lax.rem(blk(s, 1), n_e)
    e_c = lax.rem(blk(s, 2), n_e)

    def u_copy(step, slot):
        rows = pl.ds(pl.multiple_of(lax.rem(blk(step, 0), n_e) * PEER_TE, PEER_TE), PEER_TE)
        return pltpu.make_async_copy(u_hbm.at[rows, :], u_buf.at[slot], u_sem.at[slot])

    def vt_copy(step, slot):
        return pltpu.make_async_copy(vt_hbm.at[lax.rem(blk(step, 2), n_e)], vt_buf.at[slot],
                                     vt_sem.at[slot])

    @pl.when(s == 0)
    def _():
        for scr in (at0_scr, at1_scr, wt0_scr, wt1_scr):
            scr[...] = jnp.zeros(scr.shape, scr.dtype)
        u_copy(s, 0).start()
        vt_copy(s, 0).start()

    @pl.when(e_a == 0)
    def _():
        h2_s[...] = h2_ref[...]

    @pl.when(e_b == 0)
    def _():
        r2_s[...] = r2_ref[...]
        e2_s[...] = e2_ref[...]

    @pl.when(e_c == 0)
    def _():
        acc_s[...] = jnp.zeros(acc_s.shape, F32)

    def step_body(slot, at_new, at_old, wt_new, wt_old):
        @pl.when(s + 1 < n_steps)
        def _():
            u_copy(s + 1, 1 - slot).start()
            vt_copy(s + 1, 1 - slot).start()

        u_copy(s, slot).wait()
        vt_copy(s, slot).wait()
        k1_blk = c1_ref.shape[1]
        for h in range(PEER_HEADS):
            for k1 in range(k1_blk):
                shape = (SUBLANES_BF16, PEER_TM)
                c1b = jnp.broadcast_to(c1_ref[h, k1:k1 + 1, :], shape).astype(BF16)
                e1b = jnp.broadcast_to(e1_ref[h, k1:k1 + 1, :], shape).astype(BF16)
                for lb in range(PEER_TM // LANES):
                    c1b_s[lb, h * k1_blk + k1] = c1b[:, lb * LANES:(lb + 1) * LANES]
                    e1b_s[lb, h * k1_blk + k1] = e1b[:, lb * LANES:(lb + 1) * LANES]
        _dense_stages(u_buf.at[slot], vt_buf.at[slot], h2_s, c1b_s, e1b_s, r2_s, e2_s, acc_s,
                      at_new, at_old, wt_new, wt_old)

    @pl.when(lax.rem(s, 2) == 0)
    def _():
        step_body(0, at0_scr, at1_scr, wt1_scr, wt0_scr)

    @pl.when(lax.rem(s, 2) == 1)
    def _():
        step_body(1, at1_scr, at0_scr, wt0_scr, wt1_scr)

    @pl.when(jnp.logical_and(e_c == n_e - 1, s >= 2))
    def _():
        pt_ref[...] = acc_s[...]


def _peer_dense(h2, u_b, vt_b, c1, e1, r2, e2):
    t, d = h2.shape
    n_e = u_b.shape[0] // PEER_TE
    n_blocks = (t // PEER_TM) * n_e
    k1_blk = PEER_TE // N_KEYS
    blk = lambda lag: (lambda s: jnp.clip(s - lag, 0, n_blocks - 1))
    tile = lambda lag: (lambda s: blk(lag)(s) // n_e)
    eblk = lambda lag: (lambda s: lax.rem(blk(lag)(s), n_e))
    rspec = pl.BlockSpec((PEER_HEADS, k1_blk, PEER_TM), lambda s: (0, eblk(1)(s), tile(1)(s)))
    fspec = pl.BlockSpec((PEER_HEADS, N_KEYS, PEER_TM), lambda s: (0, 0, tile(1)(s)),
                         pipeline_mode=pl.Buffered(1))
    any_spec = pl.BlockSpec(memory_space=pl.ANY)
    return pl.pallas_call(
        functools.partial(_dense_kernel, n_e=n_e, n_blocks=n_blocks),
        grid=(n_blocks + 2,),
        in_specs=[pl.BlockSpec((PEER_TM, d), lambda s: (tile(0)(s), 0)),
                  any_spec, any_spec, rspec, rspec, fspec, fspec],
        out_specs=pl.BlockSpec((d, PEER_TM), lambda s: (0, tile(2)(s))),
        out_shape=jax.ShapeDtypeStruct((d, t), F32),
        scratch_shapes=[pltpu.VMEM((2, PEER_TE, d), BF16), pltpu.VMEM((2, d, PEER_TE), BF16),
                        pltpu.SemaphoreType.DMA((2,)), pltpu.SemaphoreType.DMA((2,)),
                        pltpu.VMEM((PEER_TE, PEER_TM), F32), pltpu.VMEM((PEER_TE, PEER_TM), F32),
                        pltpu.VMEM((PEER_TE, PEER_TM), BF16), pltpu.VMEM((PEER_TE, PEER_TM), BF16),
                        pltpu.VMEM((d, PEER_TM), F32), pltpu.VMEM((PEER_TM, d), BF16),
                        pltpu.VMEM((PEER_TM // LANES, PEER_HEADS * k1_blk, SUBLANES_BF16, LANES), BF16),
                        pltpu.VMEM((PEER_TM // LANES, PEER_HEADS * k1_blk, SUBLANES_BF16, LANES), BF16),
                        pltpu.VMEM((PEER_HEADS, N_KEYS, PEER_TM), BF16),
                        pltpu.VMEM((PEER_HEADS, N_KEYS, PEER_TM), BF16)],
        compiler_params=_params("arbitrary"),
        name="peer_dense",
    )(h2, u_b, vt_b, c1, e1, r2, e2)


def _final_kernel(pt_ref, x1_ref, g2_ref, lg_ref, lb_ref, o_ref):
    p = pt_ref[...].T
    o_ref[...] = _layer_norm(DN_ALPHA * x1_ref[...] + g2_ref[...] * p, lg_ref[...], lb_ref[...])


def _peer_final(pt, x1, mod4, lg, lb, seq):
    t, d = x1.shape
    tm = 512
    tpb = seq // tm
    return pl.pallas_call(
        _final_kernel,
        grid=(t // tm,),
        in_specs=[pl.BlockSpec((d, tm), lambda i: (0, i)),
                  pl.BlockSpec((tm, d), lambda i: (i, 0)),
                  pl.BlockSpec((None, None, 1, d), lambda i: (i // tpb, 5, 0, 0)),
                  pl.BlockSpec((1, d), lambda i: (0, 0)),
                  pl.BlockSpec((1, d), lambda i: (0, 0))],
        out_specs=pl.BlockSpec((tm, d), lambda i: (i, 0)),
        out_shape=jax.ShapeDtypeStruct((t, d), F32),
        compiler_params=_params("arbitrary"),
        name="peer_final",
    )(pt, x1, mod4, lg, lb)


def _half_swap(w):
    half = w.shape[-1] // 2
    return jnp.concatenate([w[..., half:], w[..., :half]], axis=-1)


def _in_proj_layout(w, pad_to):
    o = np.cumsum([0, A_WIDTH, A_WIDTH, A_WIDTH, Q_LORA, KV_LORA, QK_ROPE, pad_to, pad_to])
    qa, ka, va, cq, ckv, kr, ga, gb = [w[..., o[n]:o[n + 1]] for n in range(8)]
    z64 = jnp.zeros(kr.shape[:-1] + (LANES - QK_ROPE,), w.dtype)
    ztail = jnp.zeros(kr.shape[:-1] + (IN_TN - 2 * LANES,), w.dtype)
    return jnp.concatenate([ga, gb, qa, ka, va, cq, ckv, kr, z64, _half_swap(kr), z64, ztail], axis=-1)


def _mla_weight_layout(w_uq, w_ukv):
    r = w_uq.shape[0]
    uq = w_uq.reshape(r, B_HEADS, QK_NOPE + QK_ROPE)
    nope, rope = uq[..., :QK_NOPE], uq[..., QK_NOPE:]
    z64 = jnp.zeros((r, B_HEADS, B_HEAD_PAD - QK_NOPE - QK_ROPE), w_uq.dtype)
    wq1 = jnp.concatenate([nope, rope, z64], axis=-1).reshape(r, B_HEADS * B_HEAD_PAD)
    wq2 = jnp.concatenate([_half_swap(rope), z64], axis=-1).reshape(r, B_HEADS * LANES)
    ukv = w_ukv.reshape(w_ukv.shape[0], B_HEADS, QK_NOPE + V_HEAD)
    wk = ukv[..., :QK_NOPE].reshape(-1, B_HEADS * QK_NOPE)
    wv = ukv[..., QK_NOPE:].reshape(-1, B_WIDTH)
    return wq1.astype(BF16), wq2.astype(BF16), wk.astype(BF16), wv.astype(BF16)


def kernel(x, c, positions, w_ada, b_ada, w_in, b_in, rel_bias, q_norm_g, kv_norm_g, w_uq, w_ukv,
           w_pa, w_pb, w_o, ln1_g, ln1_b, peer_wq, peer_keys, peer_u, peer_v, ln2_g, ln2_b):
    bsz, seq, d = x.shape
    t = bsz * seq
    assert d == GB0 - GA0 and seq % 1024 == 0 and w_ada.shape[0] == 1
    x2 = x.reshape(t, d)
    row = lambda a: a.reshape(1, -1).astype(F32)

    mod4 = _ada_mod(c, w_ada[0], b_ada[0]).reshape(bsz, 6, 1, d)
    cos_t, sin_t = _rope_tables(positions)

    w2 = _in_proj_layout(w_in[0], d).astype(BF16).reshape(d, Z_WIDTH // IN_TN, IN_TN).transpose(1, 0, 2)
    b2 = _in_proj_layout(b_in[0], d).reshape(1, Z_WIDTH).astype(F32)
    z = _in_proj(x2, mod4, w2, b2, row(q_norm_g[0]), row(kv_norm_g[0]), seq)

    wq1, wq2, wk, wv = _mla_weight_layout(w_uq[0], w_ukv[0])
    qb, kb, vb = _mla_up(z, cos_t, sin_t, wq1, wq2, wk, wv)
    yb = _mla_flash(qb, kb, vb, bsz, seq)
    ya = _band_attn(z, _band_bias(rel_bias[0]), bsz, seq)

    x1, h2 = _out_proj(ya, yb, z, x2, mod4, w_pa[0].astype(BF16), w_pb[0].astype(BF16),
                       w_o[0].astype(BF16), row(ln1_g[0]), row(ln1_b[0]), seq)

    wqt = peer_wq[0].astype(BF16).T
    keys_b = peer_keys[0].reshape(2 * PEER_HEADS, N_KEYS, PEER_HALF).astype(BF16)
    c1, e1, r2, e2 = _peer_topk(h2, wqt, keys_b)
    vt_b = peer_v[0].astype(BF16).reshape(-1, PEER_TE, d).transpose(0, 2, 1)
    pt = _peer_dense(h2, peer_u[0].astype(BF16), vt_b, c1, e1, r2, e2)
    out = _peer_final(pt, x1, mod4, row(ln2_g[0]), row(ln2_b[0]), seq)
    return out.reshape(bsz, seq, d)
```

```python
import functools
import math

import jax
import jax.numpy as jnp
import numpy as np
from jax import lax
from jax.experimental import pallas as pl
from jax.experimental.pallas import tpu as pltpu

F32 = jnp.float32
BF16 = jnp.bfloat16

CHUNK = 64
A_HEADS = 8
A_HEAD_DIM = 128
A_LEFT_CHUNKS = 8
REL_CLIP_LEFT = 256
REL_CLIP_RIGHT = CHUNK - 1
B_HEADS = 8
Q_LORA = 512
KV_LORA = 512
QK_NOPE = 128
QK_ROPE = 64
V_HEAD = 128
ROPE_THETA = 10000.0
N_KEYS = 128
PEER_HEADS = 8
PEER_TOPK = 16
PEER_HALF = 128
DEPTH = 1
DN_ALPHA = (2 * DEPTH) ** 0.25
LN_EPS = 1e-5
RMS_EPS = 1e-6
NEG = -1e30

A_WIDTH = A_HEADS * A_HEAD_DIM
B_WIDTH = B_HEADS * V_HEAD
LOG2E = math.log2(math.e)
A_QSCALE = A_HEAD_DIM ** -0.5 * LOG2E
B_QSCALE = (QK_NOPE + QK_ROPE) ** -0.5 * LOG2E
SQRT_HALF = math.sqrt(0.5)

LANES = 128
SUBLANES_BF16 = 16
VMEM_LIMIT_BYTES = 56 * 1024 * 1024

IN_TN = 512
GA0 = 0
GB0 = 2048
QA0 = 4096
KA0 = 5120
VA0 = 6144
CQ0 = 7168
CKV0 = 7680
KR0 = 8192
Z_WIDTH = 8704
B_HEAD_PAD = 256


def _nt_dot(a, b):
    return lax.dot_general(a, b, (((1,), (1,)), ((), ())), preferred_element_type=F32)


def _params(*sem, flags=None, vmem=VMEM_LIMIT_BYTES):
    return pltpu.CompilerParams(dimension_semantics=sem, vmem_limit_bytes=vmem, flags=flags)


def _layer_norm(r, g, b):
    mu = jnp.mean(r, axis=-1, keepdims=True)
    d = r - mu
    var = jnp.mean(d * d, axis=-1, keepdims=True)
    return d * lax.rsqrt(var + LN_EPS) * g + b


def _ada_kernel(c_ref, w_ref, b_ref, o_ref):
    c = c_ref[...]
    cond = (c * jax.nn.sigmoid(c)).astype(BF16)
    o_ref[...] = jnp.dot(cond, w_ref[...].astype(BF16), preferred_element_type=F32) + b_ref[...]


def _ada_mod(c, w_ada, b_ada):
    bsz, d = c.shape
    n = w_ada.shape[1]
    tn = 1024
    return pl.pallas_call(
        _ada_kernel,
        grid=(n // tn,),
        in_specs=[pl.BlockSpec((bsz, d), lambda j: (0, 0)),
                  pl.BlockSpec((d, tn), lambda j: (0, j)),
                  pl.BlockSpec((1, tn), lambda j: (0, j))],
        out_specs=pl.BlockSpec((bsz, tn), lambda j: (0, j)),
        out_shape=jax.ShapeDtypeStruct((bsz, n), F32),
        compiler_params=_params("arbitrary"),
        name="ada_mod",
    )(c, w_ada, b_ada.reshape(1, n))


def _rope_kernel(pos_ref, invf_ref, cos_ref, sin_ref):
    ang = pos_ref[...] * invf_ref[...]
    cos_ref[...] = jnp.cos(ang)
    sin_ref[...] = jnp.sin(ang)


def _rope_tables(positions):
    t = positions.size
    half = QK_ROPE // 2
    per_row = LANES // half
    rows = t // per_row
    inv_freq = ROPE_THETA ** (-jnp.arange(0, QK_ROPE, 2, dtype=F32) / QK_ROPE)
    pos_rep = jnp.repeat(positions.astype(F32).reshape(rows, per_row), half, axis=1)
    invf = jnp.tile(inv_freq, per_row).reshape(1, LANES)
    tr = min(rows, 2048)
    cos, sin = pl.pallas_call(
        _rope_kernel,
        grid=(rows // tr,),
        in_specs=[pl.BlockSpec((tr, LANES), lambda i: (i, 0)),
                  pl.BlockSpec((1, LANES), lambda i: (0, 0))],
        out_specs=[pl.BlockSpec((tr, LANES), lambda i: (i, 0))] * 2,
        out_shape=[jax.ShapeDtypeStruct((rows, LANES), F32)] * 2,
        compiler_params=_params("arbitrary"),
        name="rope_tab",
    )(pos_rep, invf)
    cos = cos.reshape(t, half)
    sin = sin.reshape(t, half)
    zeros = jnp.zeros((t, LANES - QK_ROPE), F32)
    return (jnp.concatenate([cos, cos, zeros], axis=1),
            jnp.concatenate([-sin, sin, zeros], axis=1))


def _inproj_kernel(x_ref, sc_ref, sh_ref, w_ref, b_ref, gq_ref, gkv_ref, z_ref, h_scr):
    j = pl.program_id(1)

    @pl.when(j == 0)
    def _():
        h_scr[...] = (x_ref[...] * (1.0 + sc_ref[...]) + sh_ref[...]).astype(BF16)

    acc = jnp.dot(h_scr[...], w_ref[...], preferred_element_type=F32) + b_ref[...]

    def rms(g):
        ms = jnp.mean(acc * acc, axis=-1, keepdims=True)
        return acc * lax.rsqrt(ms + RMS_EPS) * g

    @pl.when(j < QA0 // IN_TN)
    def _():
        z_ref[...] = jax.nn.sigmoid(acc).astype(BF16)

    @pl.when(jnp.logical_and(j >= QA0 // IN_TN, j < KA0 // IN_TN))
    def _():
        z_ref[...] = (acc * A_QSCALE).astype(BF16)

    @pl.when(jnp.logical_or(jnp.logical_and(j >= KA0 // IN_TN, j < CQ0 // IN_TN), j == KR0 // IN_TN))
    def _():
        z_ref[...] = acc.astype(BF16)

    @pl.when(j == CQ0 // IN_TN)
    def _():
        z_ref[...] = rms(gq_ref[...]).astype(BF16)

    @pl.when(j == CKV0 // IN_TN)
    def _():
        z_ref[...] = rms(gkv_ref[...]).astype(BF16)


def _in_proj(x2, mod4, w2, b2, gq, gkv, seq):
    t, d = x2.shape
    tm = min(1024, seq)
    tpb = seq // tm
    return pl.pallas_call(
        _inproj_kernel,
        grid=(t // tm, Z_WIDTH // IN_TN),
        in_specs=[pl.BlockSpec((tm, d), lambda i, j: (i, 0)),
                  pl.BlockSpec((None, None, 1, d), lambda i, j: (i // tpb, 1, 0, 0)),
                  pl.BlockSpec((None, None, 1, d), lambda i, j: (i // tpb, 0, 0, 0)),
                  pl.BlockSpec((None, d, IN_TN), lambda i, j: (j, 0, 0)),
                  pl.BlockSpec((1, IN_TN), lambda i, j: (0, j)),
                  pl.BlockSpec((1, Q_LORA), lambda i, j: (0, 0)),
                  pl.BlockSpec((1, KV_LORA), lambda i, j: (0, 0))],
        out_specs=pl.BlockSpec((tm, IN_TN), lambda i, j: (i, j)),
        out_shape=jax.ShapeDtypeStruct((t, Z_WIDTH), BF16),
        scratch_shapes=[pltpu.VMEM((tm, d), BF16)],
        compiler_params=_params("arbitrary", "arbitrary"),
        name="in_proj",
    )(x2, mod4, mod4, w2, b2, gq, gkv)


FLASH_HEADS = 2


def _mla_up_kernel(cq_ref, ckv_ref, kra_ref, krb_ref, cos_ref, sin_ref,
                   wq1_ref, wq2_ref, wk_ref, wv_ref, q_ref, k_ref, v_ref):
    cq = cq_ref[...]
    ckv = ckv_ref[...]
    cos = cos_ref[...]
    sin = sin_ref[...]
    qa = jnp.dot(cq, wq1_ref[...], preferred_element_type=F32)
    qb = jnp.dot(cq, wq2_ref[...], preferred_element_type=F32)
    kn = jnp.dot(ckv, wk_ref[...], preferred_element_type=F32)
    vv = jnp.dot(ckv, wv_ref[...], preferred_element_type=F32).astype(BF16)
    k_rope = (kra_ref[...].astype(F32) * cos + krb_ref[...].astype(F32) * sin).astype(BF16)
    for h in range(B_HEADS):
        c0 = h * B_HEAD_PAD
        n0 = h * QK_NOPE
        grp = h // FLASH_HEADS
        o0 = (h % FLASH_HEADS) * B_HEAD_PAD
        v0 = (h % FLASH_HEADS) * V_HEAD
        q_ref[grp, :, o0:o0 + QK_NOPE] = (qa[:, c0:c0 + QK_NOPE] * B_QSCALE).astype(BF16)
        q_rope = qa[:, c0 + QK_NOPE:c0 + B_HEAD_PAD] * cos + qb[:, n0:n0 + QK_NOPE] * sin
        q_ref[grp, :, o0 + QK_NOPE:o0 + B_HEAD_PAD] = (q_rope * B_QSCALE).astype(BF16)
        k_ref[grp, :, o0:o0 + QK_NOPE] = kn[:, n0:n0 + QK_NOPE].astype(BF16)
        k_ref[grp, :, o0 + QK_NOPE:o0 + B_HEAD_PAD] = k_rope
        v_ref[grp, :, v0:v0 + V_HEAD] = vv[:, n0:n0 + V_HEAD]


def _mla_up(z, cos_t, sin_t, wq1, wq2, wk, wv):
    t = z.shape[0]
    tm = 512
    groups = B_HEADS // FLASH_HEADS
    const = lambda i: (0, 0)
    return pl.pallas_call(
        _mla_up_kernel,
        grid=(t // tm,),
        in_specs=[pl.BlockSpec((tm, Q_LORA), lambda i: (i, CQ0 // Q_LORA)),
                  pl.BlockSpec((tm, KV_LORA), lambda i: (i, CKV0 // KV_LORA)),
                  pl.BlockSpec((tm, LANES), lambda i: (i, KR0 // LANES)),
                  pl.BlockSpec((tm, LANES), lambda i: (i, KR0 // LANES + 1)),
                  pl.BlockSpec((tm, LANES), lambda i: (i, 0)),
                  pl.BlockSpec((tm, LANES), lambda i: (i, 0)),
                  pl.BlockSpec(wq1.shape, const),
                  pl.BlockSpec(wq2.shape, const),
                  pl.BlockSpec(wk.shape, const),
                  pl.BlockSpec(wv.shape, const)],
        out_specs=[pl.BlockSpec((groups, tm, FLASH_HEADS * B_HEAD_PAD), lambda i: (0, i, 0)),
                   pl.BlockSpec((groups, tm, FLASH_HEADS * B_HEAD_PAD), lambda i: (0, i, 0)),
                   pl.BlockSpec((groups, tm, FLASH_HEADS * V_HEAD), lambda i: (0, i, 0))],
        out_shape=[jax.ShapeDtypeStruct((groups, t, FLASH_HEADS * B_HEAD_PAD), BF16),
                   jax.ShapeDtypeStruct((groups, t, FLASH_HEADS * B_HEAD_PAD), BF16),
                   jax.ShapeDtypeStruct((groups, t, FLASH_HEADS * V_HEAD), BF16)],
        compiler_params=_params("arbitrary"),
        name="mla_up",
    )(z, z, z, z, cos_t, sin_t, wq1, wq2, wk, wv)


def _flash_kernel(qi_ref, kj_ref, q_ref, k_ref, v_ref, o_ref, q_s, k_s, v_s, m_scr, l_scr, acc_scr):
    p = pl.program_id(2)
    i = qi_ref[p]
    j = kj_ref[p]
    tq = q_ref.shape[0]
    tk = k_ref.shape[0]

    @pl.when(j == 0)
    def _():
        m_scr[...] = jnp.full(m_scr.shape, -jnp.inf, F32)
        l_scr[...] = jnp.zeros(l_scr.shape, F32)
        acc_scr[...] = jnp.zeros(acc_scr.shape, F32)

    def step(diagonal):
        q_s[...] = q_ref[...]
        k_s[...] = k_ref[...]
        v_s[...] = v_ref[...]
        for hh in range(FLASH_HEADS):
            qk = slice(hh * B_HEAD_PAD, (hh + 1) * B_HEAD_PAD)
            vv = slice(hh * V_HEAD, (hh + 1) * V_HEAD)
            s = _nt_dot(q_s[:, qk], k_s[:, qk])
            if diagonal:
                qpos = lax.broadcasted_iota(jnp.int32, (tq, tk), 0)
                kpos = lax.broadcasted_iota(jnp.int32, (tq, tk), 1)
                s = jnp.where(kpos <= (qpos | (CHUNK - 1)), s, NEG)
            m_prev = m_scr[hh]
            m_new = jnp.maximum(m_prev, jnp.max(s, axis=1, keepdims=True))
            alpha = jnp.exp2(m_prev - m_new)
            pexp = jnp.exp2(s - m_new)
            l_new = alpha * l_scr[hh] + jnp.sum(pexp, axis=1, keepdims=True)
            acc = alpha * acc_scr[hh] + jnp.dot(pexp.astype(BF16), v_s[:, vv],
                                                preferred_element_type=F32)
            m_scr[hh] = m_new
            l_scr[hh] = l_new
            acc_scr[hh] = acc
            if diagonal:
                o_ref[:, vv] = (acc / l_new).astype(BF16)

    @pl.when(j < i)
    def _():
        step(False)

    @pl.when(j == i)
    def _():
        step(True)


def _mla_flash(q, k, v, bsz, seq):
    tq = min(1024, seq)
    nqb = seq // tq
    pairs = [(i, j) for i in range(nqb) for j in range(i + 1)]
    qi = jnp.asarray([p[0] for p in pairs], jnp.int32)
    kj = jnp.asarray([p[1] for p in pairs], jnp.int32)
    qk_w = FLASH_HEADS * B_HEAD_PAD
    v_w = FLASH_HEADS * V_HEAD
    grid_spec = pltpu.PrefetchScalarGridSpec(
        num_scalar_prefetch=2,
        grid=(bsz, B_HEADS // FLASH_HEADS, len(pairs)),
        in_specs=[pl.BlockSpec((None, tq, qk_w), lambda b, h, p, qi, kj: (h, b * nqb + qi[p], 0)),
                  pl.BlockSpec((None, tq, qk_w), lambda b, h, p, qi, kj: (h, b * nqb + kj[p], 0)),
                  pl.BlockSpec((None, tq, v_w), lambda b, h, p, qi, kj: (h, b * nqb + kj[p], 0))],
        out_specs=pl.BlockSpec((tq, v_w), lambda b, h, p, qi, kj: (b * nqb + qi[p], h)),
        scratch_shapes=[pltpu.VMEM((tq, qk_w), BF16), pltpu.VMEM((tq, qk_w), BF16),
                        pltpu.VMEM((tq, v_w), BF16),
                        pltpu.VMEM((FLASH_HEADS, tq, 1), F32), pltpu.VMEM((FLASH_HEADS, tq, 1), F32),
                        pltpu.VMEM((FLASH_HEADS, tq, V_HEAD), F32)],
    )
    return pl.pallas_call(
        _flash_kernel,
        grid_spec=grid_spec,
        out_shape=jax.ShapeDtypeStruct((bsz * seq, B_WIDTH), BF16),
        compiler_params=_params("arbitrary", "arbitrary", "arbitrary"),
        name="mla_flash",
    )(qi, kj, q, k, v)


BAND_TQ = A_LEFT_CHUNKS * CHUNK


BAND_HEADS = 2


def _band_kernel(q_ref, kp_ref, kc_ref, vp_ref, vc_ref, bias_ref, o_ref):
    i = pl.program_id(2)
    first = jnp.where(i == 0, NEG, 0.0).astype(F32)
    outs = []
    for hh in range(BAND_HEADS):
        cols = slice(hh * A_HEAD_DIM, (hh + 1) * A_HEAD_DIM)
        q = q_ref[:, cols]
        s_prev = _nt_dot(q, kp_ref[:, cols]) + bias_ref[hh, :, :BAND_TQ] + first
        s_cur = _nt_dot(q, kc_ref[:, cols]) + bias_ref[hh, :, BAND_TQ:]
        m = jnp.maximum(jnp.max(s_prev, axis=1, keepdims=True), jnp.max(s_cur, axis=1, keepdims=True))
        p_prev = jnp.exp2(s_prev - m)
        p_cur = jnp.exp2(s_cur - m)
        l = jnp.sum(p_prev, axis=1, keepdims=True) + jnp.sum(p_cur, axis=1, keepdims=True)
        o = (jnp.dot(p_prev.astype(BF16), vp_ref[:, cols], preferred_element_type=F32)
             + jnp.dot(p_cur.astype(BF16), vc_ref[:, cols], preferred_element_type=F32))
        outs.append((o / l).astype(BF16))
    for hh in range(BAND_HEADS):
        o_ref[:, hh * A_HEAD_DIM:(hh + 1) * A_HEAD_DIM] = outs[hh]


def _bias_kernel(base_ref, o_ref):
    rows = jnp.broadcast_to(base_ref[...], (BAND_TQ, base_ref.shape[-1]))
    toep = pltpu.roll(rows, 0, 1, stride=1, stride_axis=0)[:, :2 * BAND_TQ]
    qc = lax.broadcasted_iota(jnp.int32, toep.shape, 0) // CHUNK + A_LEFT_CHUNKS
    kc = lax.broadcasted_iota(jnp.int32, toep.shape, 1) // CHUNK
    valid = jnp.logical_and(kc <= qc, kc >= qc - A_LEFT_CHUNKS)
    o_ref[...] = jnp.where(valid, toep * LOG2E, NEG)


def _band_bias(rel_bias):
    period = 3 * BAND_TQ
    heads, n_rel = rel_bias.shape
    rep = lambda col, n: jnp.broadcast_to(rel_bias[:, col:col + 1], (heads, n))
    n_hi = BAND_TQ - REL_CLIP_LEFT
    base = jnp.concatenate([rep(n_rel - 1, n_hi), rel_bias[:, ::-1],
                            rep(0, 2 * BAND_TQ + 1 - n_hi - n_rel), rep(n_rel - 1, BAND_TQ - 1)],
                           axis=1).astype(F32).reshape(heads, 1, period)
    return pl.pallas_call(
        _bias_kernel,
        grid=(heads,),
        in_specs=[pl.BlockSpec((None, 1, period), lambda h: (h, 0, 0))],
        out_specs=pl.BlockSpec((None, BAND_TQ, 2 * BAND_TQ), lambda h: (h, 0, 0)),
        out_shape=jax.ShapeDtypeStruct((heads, BAND_TQ, 2 * BAND_TQ), F32),
        compiler_params=_params("arbitrary"),
        name="band_bias",
    )(base)


def _band_attn(z, bias, bsz, seq):
    nq = seq // BAND_TQ
    width = BAND_HEADS * A_HEAD_DIM
    qb, kb, vb = QA0 // width, KA0 // width, VA0 // width
    cur = lambda c0: (lambda h, b, i: (b * nq + i, c0 + h))
    prev = lambda c0: (lambda h, b, i: (b * nq + jnp.maximum(i - 1, 0), c0 + h))
    blk = (BAND_TQ, width)
    return pl.pallas_call(
        _band_kernel,
        grid=(A_HEADS // BAND_HEADS, bsz, nq),
        in_specs=[pl.BlockSpec(blk, cur(qb)),
                  pl.BlockSpec(blk, prev(kb)), pl.BlockSpec(blk, cur(kb)),
                  pl.BlockSpec(blk, prev(vb)), pl.BlockSpec(blk, cur(vb)),
                  pl.BlockSpec((BAND_HEADS, BAND_TQ, 2 * BAND_TQ), lambda h, b, i: (h, 0, 0))],
        out_specs=pl.BlockSpec(blk, lambda h, b, i: (b * nq + i, h)),
        out_shape=jax.ShapeDtypeStruct((bsz * seq, A_WIDTH), BF16),
        compiler_params=_params("arbitrary", "arbitrary", "arbitrary"),
        name="band_attn",
    )(z, z, z, z, z, bias)


def _outproj_kernel(ya_ref, yb_ref, ga_ref, gb_ref, x_ref, g1_ref, sc2_ref, sh2_ref,
                    wpa_ref, wpb_ref, wo_ref, lg_ref, lb_ref, x1_ref, h2_ref):
    ta = jnp.dot(ya_ref[...], wpa_ref[...], preferred_element_type=F32)
    tb = jnp.dot(yb_ref[...], wpb_ref[...], preferred_element_type=F32)
    u = ga_ref[...].astype(F32) * ta + gb_ref[...].astype(F32) * tb
    y = jnp.dot(u.astype(BF16), wo_ref[...], preferred_element_type=F32)
    x1 = _layer_norm(DN_ALPHA * x_ref[...] + g1_ref[...] * y, lg_ref[...], lb_ref[...])
    x1_ref[...] = x1
    h2_ref[...] = (x1 * (1.0 + sc2_ref[...]) + sh2_ref[...]).astype(BF16)


def _out_proj(ya, yb, z, x2, mod4, wpa, wpb, wo, lg, lb, seq):
    t, d = x2.shape
    tm = 256
    tpb = seq // tm
    const = lambda i: (0, 0)
    modspec = lambda k: pl.BlockSpec((None, None, 1, d), lambda i: (i // tpb, k, 0, 0))
    return pl.pallas_call(
        _outproj_kernel,
        grid=(t // tm,),
        in_specs=[pl.BlockSpec((tm, A_WIDTH), lambda i: (i, 0)),
                  pl.BlockSpec((tm, B_WIDTH), lambda i: (i, 0)),
                  pl.BlockSpec((tm, d), lambda i: (i, GA0 // d)),
                  pl.BlockSpec((tm, d), lambda i: (i, GB0 // d)),
                  pl.BlockSpec((tm, d), lambda i: (i, 0)),
                  modspec(2), modspec(4), modspec(3),
                  pl.BlockSpec(wpa.shape, const), pl.BlockSpec(wpb.shape, const),
                  pl.BlockSpec(wo.shape, const),
                  pl.BlockSpec((1, d), const), pl.BlockSpec((1, d), const)],
        out_specs=[pl.BlockSpec((tm, d), lambda i: (i, 0)),
                   pl.BlockSpec((tm, d), lambda i: (i, 0))],
        out_shape=[jax.ShapeDtypeStruct((t, d), F32), jax.ShapeDtypeStruct((t, d), BF16)],
        compiler_params=_params("arbitrary"),
        name="out_proj",
    )(ya, yb, z, z, x2, mod4, mod4, mod4, wpa, wpb, wo, lg, lb)


TOPV = PEER_TOPK + 1
TOPV_ROWS = 24
SUBLANES_F32 = 8


def _sort_network(n):
    def merge(lo, hi, r):
        step = r * 2
        if step < hi - lo:
            yield from merge(lo, hi, step)
            yield from merge(lo + r, hi, step)
            yield from [(i, i + r) for i in range(lo + r, hi - r, step)]
        else:
            yield (lo, lo + r)

    def sort(lo, hi):
        if hi - lo >= 1:
            mid = lo + (hi - lo) // 2
            yield from sort(lo, mid)
            yield from sort(mid + 1, hi)
            yield from merge(lo, hi, 1)

    return list(sort(0, n - 1))


def _pop_top(lists, singles, out_scr):
    depth = len(lists)
    for r in range(TOPV):
        head = lists[0]
        for x in singles:
            head = jnp.maximum(head, x)
        m = jnp.max(head, axis=0, keepdims=True)
        out_scr[r:r + 1, :] = m
        remaining = TOPV - 1 - r
        if remaining == 0:
            break
        hit = lists[0] == m
        for k in range(min(depth, remaining)):
            nxt = lists[k + 1] if k + 1 < depth else -jnp.inf
            lists[k] = jnp.where(hit, nxt, lists[k])
        singles = [jnp.where(x == m, -jnp.inf, x) for x in singles]


def _top_vals(s, out_scr):
    n = s.shape[0] // SUBLANES_F32
    lists = [s[k * SUBLANES_F32:(k + 1) * SUBLANES_F32, :] for k in range(n)]
    for a, b in _sort_network(n):
        lists[a], lists[b] = jnp.maximum(lists[a], lists[b]), jnp.minimum(lists[a], lists[b])
    _pop_top(lists, [], out_scr)


def _topk_kernel(h2_ref, wqt_ref, keys_ref, c1_ref, e1_ref, r2_ref, e2_ref,
                 st_scr, tv1_scr, tv2_scr, fv_scr):
    pqt = _nt_dot(wqt_ref[...], h2_ref[...]).astype(BF16)
    for hp in range(2 * PEER_HEADS):
        r0 = hp * PEER_HALF
        st_scr[r0:r0 + N_KEYS, :] = jnp.dot(keys_ref[hp], pqt[r0:r0 + PEER_HALF, :],
                                            preferred_element_type=F32)
    pad = jnp.full((TOPV_ROWS - PEER_TOPK, h2_ref.shape[0]), -jnp.inf, F32)
    tv1_scr[PEER_TOPK:, :] = pad
    tv2_scr[PEER_TOPK:, :] = pad

    def head(h, carry):
        r1 = pl.multiple_of(h * 2 * N_KEYS, 2 * N_KEYS)
        s1 = st_scr[pl.ds(r1, N_KEYS), :]
        s2 = st_scr[pl.ds(r1 + N_KEYS, N_KEYS), :]
        _top_vals(s1, tv1_scr)
        _top_vals(s2, tv2_scr)
        tv1_lo = tv1_scr[0:SUBLANES_F32, :]
        lists = [tv1_lo + tv2_scr[j:j + 1, :] for j in range(PEER_TOPK)]
        singles = [tv1_scr[SUBLANES_F32:PEER_TOPK, :] + tv2_scr[0:1, :],
                   tv1_scr[PEER_TOPK:, :] + tv2_scr[0:1, :],
                   tv2_scr[PEER_TOPK:, :] + tv1_scr[0:1, :]]
        _pop_top(lists, singles, fv_scr)
        top = fv_scr[0:1, :]
        kept = fv_scr[0:PEER_TOPK, :]
        theta = 0.5 * (fv_scr[PEER_TOPK - 1:PEER_TOPK, :] + fv_scr[PEER_TOPK:PEER_TOPK + 1, :])
        inv_z = 1.0 / jnp.sum(jnp.exp(kept - top), axis=0, keepdims=True)
        tau = theta - s1
        count1 = jnp.zeros_like(s1)
        rank2 = jnp.zeros_like(s2)
        for j in range(TOPV):
            tv2_j = tv2_scr[j:j + 1, :]
            count1 = jnp.where(tv2_j >= tau, float(j + 1), count1)
            rank2 = jnp.where(tv2_j > s2, float(j + 1), rank2)
        c1_ref[h] = count1
        e1_ref[h] = jnp.exp(s1 - tv1_scr[0:1, :]) * inv_z
        r2_ref[h] = rank2.astype(BF16)
        e2_ref[h] = jnp.exp(s2 - tv2_scr[0:1, :]).astype(BF16)
        return carry

    lax.fori_loop(0, PEER_HEADS, head, 0)


def _peer_topk(h2, wqt, keys_b):
    t, d = h2.shape
    tm = 512
    rows = PEER_HEADS * 2 * N_KEYS
    out = lambda dt: jax.ShapeDtypeStruct((PEER_HEADS, N_KEYS, t), dt)
    ospec = pl.BlockSpec((PEER_HEADS, N_KEYS, tm), lambda i: (0, 0, i))
    return pl.pallas_call(
        _topk_kernel,
        grid=(t // tm,),
        in_specs=[pl.BlockSpec((tm, d), lambda i: (i, 0)),
                  pl.BlockSpec(wqt.shape, lambda i: (0, 0)),
                  pl.BlockSpec(keys_b.shape, lambda i: (0, 0, 0))],
        out_specs=[ospec] * 4,
        out_shape=[out(F32), out(F32), out(BF16), out(BF16)],
        scratch_shapes=[pltpu.VMEM((rows, tm), F32),
                        pltpu.VMEM((TOPV_ROWS, tm), F32),
                        pltpu.VMEM((TOPV_ROWS, tm), F32),
                        pltpu.VMEM((TOPV_ROWS, tm), F32)],
        compiler_params=_params("arbitrary"),
        name="peer_topk",
    )(h2, wqt, keys_b)


PEER_TE = 1024
PEER_TM = 512
def _dense_kernel(h2_ref, u_ref, vt_ref, c1_ref, e1_ref, r2_ref, e2_ref, pt_ref,
                  at_s, wt_s, c1b_s, e1b_s):
    e = pl.program_id(1)
    rows = SUBLANES_BF16
    k1_blk = PEER_TE // N_KEYS
    n_lb = PEER_TM // LANES

    @pl.when(e == 0)
    def _():
        pt_ref[...] = jnp.zeros(pt_ref.shape, F32)

    for h in range(PEER_HEADS):
        for k1 in range(k1_blk):
            shape = (rows, PEER_TM)
            c1b = jnp.broadcast_to(c1_ref[h, k1:k1 + 1, :], shape).astype(BF16)
            e1b = jnp.broadcast_to(e1_ref[h, k1:k1 + 1, :], shape).astype(BF16)
            for lb in range(n_lb):
                c1b_s[lb, h * k1_blk + k1] = c1b[:, lb * LANES:(lb + 1) * LANES]
                e1b_s[lb, h * k1_blk + k1] = e1b[:, lb * LANES:(lb + 1) * LANES]

    at_s[...] = _nt_dot(u_ref[...], h2_ref[...])
    for lb in range(n_lb):
        lanes = slice(lb * LANES, (lb + 1) * LANES)
        for g in range(N_KEYS // rows):
            krows = slice(g * rows, (g + 1) * rows)
            gates = [jnp.zeros((rows, LANES), BF16) for _ in range(k1_blk)]
            for h in range(PEER_HEADS):
                r2 = r2_ref[h, krows, lanes]
                e2 = e2_ref[h, krows, lanes]
                for k1 in range(k1_blk):
                    c1 = c1b_s[lb, h * k1_blk + k1]
                    e1 = e1b_s[lb, h * k1_blk + k1]
                    gates[k1] = gates[k1] + jnp.where(r2 < c1, e2, jnp.zeros_like(e2)) * e1
            tiles = []
            for k1 in range(k1_blk):
                r0 = k1 * N_KEYS + g * rows
                a = at_s[r0:r0 + rows, lanes]
                gelu = 0.5 * a * (1.0 + lax.erf(a * SQRT_HALF))
                tiles.append((r0, gelu.astype(BF16) * gates[k1]))
            for r0, w in tiles:
                wt_s[r0:r0 + rows, lanes] = w
    pt_ref[...] += jnp.dot(vt_ref[...], wt_s[...], preferred_element_type=F32)


def _peer_dense(h2, u_b, vt_b, c1, e1, r2, e2):
    t, d = h2.shape
    n_e = u_b.shape[0] // PEER_TE
    k1_blk = PEER_TE // N_KEYS
    rspec = pl.BlockSpec((PEER_HEADS, k1_blk, PEER_TM), lambda i, e: (0, e, i))
    fspec = pl.BlockSpec((PEER_HEADS, N_KEYS, PEER_TM), lambda i, e: (0, 0, i))
    tile_shape = (PEER_TM // LANES, PEER_HEADS * k1_blk, SUBLANES_BF16, LANES)
    return pl.pallas_call(
        _dense_kernel,
        grid=(t // PEER_TM, n_e),
        in_specs=[pl.BlockSpec((PEER_TM, d), lambda i, e: (i, 0)),
                  pl.BlockSpec((PEER_TE, d), lambda i, e: (e, 0)),
                  pl.BlockSpec((None, d, PEER_TE), lambda i, e: (e, 0, 0)),
                  rspec, rspec, fspec, fspec],
        out_specs=pl.BlockSpec((d, PEER_TM), lambda i, e: (0, i)),
        out_shape=jax.ShapeDtypeStruct((d, t), F32),
        scratch_shapes=[pltpu.VMEM((PEER_TE, PEER_TM), F32), pltpu.VMEM((PEER_TE, PEER_TM), BF16),
                        pltpu.VMEM(tile_shape, BF16), pltpu.VMEM(tile_shape, BF16)],
        compiler_params=_params("arbitrary", "arbitrary"),
        name="peer_dense",
    )(h2, u_b, vt_b, c1, e1, r2, e2)


def _final_kernel(pt_ref, x1_ref, g2_ref, lg_ref, lb_ref, o_ref):
    p = pt_ref[...].T
    o_ref[...] = _layer_norm(DN_ALPHA * x1_ref[...] + g2_ref[...] * p, lg_ref[...], lb_ref[...])


def _peer_final(pt, x1, mod4, lg, lb, seq):
    t, d = x1.shape
    tm = 512
    tpb = seq // tm
    return pl.pallas_call(
        _final_kernel,
        grid=(t // tm,),
        in_specs=[pl.BlockSpec((d, tm), lambda i: (0, i)),
                  pl.BlockSpec((tm, d), lambda i: (i, 0)),
                  pl.BlockSpec((None, None, 1, d), lambda i: (i // tpb, 5, 0, 0)),
                  pl.BlockSpec((1, d), lambda i: (0, 0)),
                  pl.BlockSpec((1, d), lambda i: (0, 0))],
        out_specs=pl.BlockSpec((tm, d), lambda i: (i, 0)),
        out_shape=jax.ShapeDtypeStruct((t, d), F32),
        compiler_params=_params("arbitrary"),
        name="peer_final",
    )(pt, x1, mod4, lg, lb)


def _half_swap(w):
    half = w.shape[-1] // 2
    return jnp.concatenate([w[..., half:], w[..., :half]], axis=-1)


def _in_proj_layout(w, pad_to):
    o = np.cumsum([0, A_WIDTH, A_WIDTH, A_WIDTH, Q_LORA, KV_LORA, QK_ROPE, pad_to, pad_to])
    qa, ka, va, cq, ckv, kr, ga, gb = [w[..., o[n]:o[n + 1]] for n in range(8)]
    z64 = jnp.zeros(kr.shape[:-1] + (LANES - QK_ROPE,), w.dtype)
    ztail = jnp.zeros(kr.shape[:-1] + (IN_TN - 2 * LANES,), w.dtype)
    return jnp.concatenate([ga, gb, qa, ka, va, cq, ckv, kr, z64, _half_swap(kr), z64, ztail], axis=-1)


def _mla_weight_layout(w_uq, w_ukv):
    r = w_uq.shape[0]
    uq = w_uq.reshape(r, B_HEADS, QK_NOPE + QK_ROPE)
    nope, rope = uq[..., :QK_NOPE], uq[..., QK_NOPE:]
    z64 = jnp.zeros((r, B_HEADS, B_HEAD_PAD - QK_NOPE - QK_ROPE), w_uq.dtype)
    wq1 = jnp.concatenate([nope, rope, z64], axis=-1).reshape(r, B_HEADS * B_HEAD_PAD)
    wq2 = jnp.concatenate([_half_swap(rope), z64], axis=-1).reshape(r, B_HEADS * LANES)
    ukv = w_ukv.reshape(w_ukv.shape[0], B_HEADS, QK_NOPE + V_HEAD)
    wk = ukv[..., :QK_NOPE].reshape(-1, B_HEADS * QK_NOPE)
    wv = ukv[..., QK_NOPE:].reshape(-1, B_WIDTH)
    return wq1.astype(BF16), wq2.astype(BF16), wk.astype(BF16), wv.astype(BF16)


def kernel(x, c, positions, w_ada, b_ada, w_in, b_in, rel_bias, q_norm_g, kv_norm_g, w_uq, w_ukv,
           w_pa, w_pb, w_o, ln1_g, ln1_b, peer_wq, peer_keys, peer_u, peer_v, ln2_g, ln2_b):
    bsz, seq, d = x.shape
    t = bsz * seq
    assert d == GB0 - GA0 and seq % 1024 == 0 and w_ada.shape[0] == 1
    x2 = x.reshape(t, d)
    row = lambda a: a.reshape(1, -1).astype(F32)

    mod4 = _ada_mod(c, w_ada[0], b_ada[0]).reshape(bsz, 6, 1, d)
    cos_t, sin_t = _rope_tables(positions)

    w2 = _in_proj_layout(w_in[0], d).astype(BF16).reshape(d, Z_WIDTH // IN_TN, IN_TN).transpose(1, 0, 2)
    b2 = _in_proj_layout(b_in[0], d).reshape(1, Z_WIDTH).astype(F32)
    z = _in_proj(x2, mod4, w2, b2, row(q_norm_g[0]), row(kv_norm_g[0]), seq)

    wq1, wq2, wk, wv = _mla_weight_layout(w_uq[0], w_ukv[0])
    qb, kb, vb = _mla_up(z, cos_t, sin_t, wq1, wq2, wk, wv)
    yb = _mla_flash(qb, kb, vb, bsz, seq)
    ya = _band_attn(z, _band_bias(rel_bias[0]), bsz, seq)

    x1, h2 = _out_proj(ya, yb, z, x2, mod4, w_pa[0].astype(BF16), w_pb[0].astype(BF16),
                       w_o[0].astype(BF16), row(ln1_g[0]), row(ln1_b[0]), seq)

    wqt = peer_wq[0].astype(BF16).T
    keys_b = peer_keys[0].reshape(2 * PEER_HEADS, N_KEYS, PEER_HALF).astype(BF16)
    c1, e1, r2, e2 = _peer_topk(h2, wqt, keys_b)
    vt_b = peer_v[0].astype(BF16).reshape(-1, PEER_TE, d).transpose(0, 2, 1)
    pt = _peer_dense(h2, peer_u[0].astype(BF16), vt_b, c1, e1, r2, e2)
    out = _peer_final(pt, x1, mod4, row(ln2_g[0]), row(ln2_b[0]), seq)
    return out.reshape(bsz, seq, d)
```

```python
import functools
import math

import jax
import jax.numpy as jnp
import numpy as np
from jax import lax
from jax.experimental import pallas as pl
from jax.experimental.pallas import tpu as pltpu

F32 = jnp.float32
BF16 = jnp.bfloat16

CHUNK = 64
A_HEADS = 8
A_HEAD_DIM = 128
A_LEFT_CHUNKS = 8
REL_CLIP_LEFT = 256
REL_CLIP_RIGHT = CHUNK - 1
B_HEADS = 8
Q_LORA = 512
KV_LORA = 512
QK_NOPE = 128
QK_ROPE = 64
V_HEAD = 128
ROPE_THETA = 10000.0
N_KEYS = 128
PEER_HEADS = 8
PEER_TOPK = 16
PEER_HALF = 128
DEPTH = 1
DN_ALPHA = (2 * DEPTH) ** 0.25
LN_EPS = 1e-5
RMS_EPS = 1e-6
NEG = -1e30

A_WIDTH = A_HEADS * A_HEAD_DIM
B_WIDTH = B_HEADS * V_HEAD
LOG2E = math.log2(math.e)
A_QSCALE = A_HEAD_DIM ** -0.5 * LOG2E
B_QSCALE = (QK_NOPE + QK_ROPE) ** -0.5 * LOG2E
SQRT_HALF = math.sqrt(0.5)

LANES = 128
SUBLANES_BF16 = 16
VMEM_LIMIT_BYTES = 56 * 1024 * 1024

IN_TN = 512
GA0 = 0
GB0 = 2048
QA0 = 4096
KA0 = 5120
VA0 = 6144
CQ0 = 7168
CKV0 = 7680
KR0 = 8192
Z_WIDTH = 8704
B_HEAD_PAD = 256


def _nt_dot(a, b):
    return lax.dot_general(a, b, (((1,), (1,)), ((), ())), preferred_element_type=F32)


def _params(*sem, flags=None, vmem=VMEM_LIMIT_BYTES):
    return pltpu.CompilerParams(dimension_semantics=sem, vmem_limit_bytes=vmem, flags=flags)


def _layer_norm(r, g, b):
    mu = jnp.mean(r, axis=-1, keepdims=True)
    d = r - mu
    var = jnp.mean(d * d, axis=-1, keepdims=True)
    return d * lax.rsqrt(var + LN_EPS) * g + b


def _ada_kernel(c_ref, w_ref, b_ref, o_ref):
    c = c_ref[...]
    cond = (c * jax.nn.sigmoid(c)).astype(BF16)
    o_ref[...] = jnp.dot(cond, w_ref[...].astype(BF16), preferred_element_type=F32) + b_ref[...]


def _ada_mod(c, w_ada, b_ada):
    bsz, d = c.shape
    n = w_ada.shape[1]
    tn = 1024
    return pl.pallas_call(
        _ada_kernel,
        grid=(n // tn,),
        in_specs=[pl.BlockSpec((bsz, d), lambda j: (0, 0)),
                  pl.BlockSpec((d, tn), lambda j: (0, j)),
                  pl.BlockSpec((1, tn), lambda j: (0, j))],
        out_specs=pl.BlockSpec((bsz, tn), lambda j: (0, j)),
        out_shape=jax.ShapeDtypeStruct((bsz, n), F32),
        compiler_params=_params("arbitrary"),
        name="ada_mod",
    )(c, w_ada, b_ada.reshape(1, n))


def _rope_kernel(pos_ref, invf_ref, cos_ref, sin_ref):
    ang = pos_ref[...] * invf_ref[...]
    cos_ref[...] = jnp.cos(ang)
    sin_ref[...] = jnp.sin(ang)


def _rope_tables(positions):
    t = positions.size
    half = QK_ROPE // 2
    per_row = LANES // half
    rows = t // per_row
    inv_freq = ROPE_THETA ** (-jnp.arange(0, QK_ROPE, 2, dtype=F32) / QK_ROPE)
    pos_rep = jnp.repeat(positions.astype(F32).reshape(rows, per_row), half, axis=1)
    invf = jnp.tile(inv_freq, per_row).reshape(1, LANES)
    tr = min(rows, 2048)
    cos, sin = pl.pallas_call(
        _rope_kernel,
        grid=(rows // tr,),
        in_specs=[pl.BlockSpec((tr, LANES), lambda i: (i, 0)),
                  pl.BlockSpec((1, LANES), lambda i: (0, 0))],
        out_specs=[pl.BlockSpec((tr, LANES), lambda i: (i, 0))] * 2,
        out_shape=[jax.ShapeDtypeStruct((rows, LANES), F32)] * 2,
        compiler_params=_params("arbitrary"),
        name="rope_tab",
    )(pos_rep, invf)
    cos = cos.reshape(t, half)
    sin = sin.reshape(t, half)
    zeros = jnp.zeros((t, LANES - QK_ROPE), F32)
    return (jnp.concatenate([cos, cos, zeros], axis=1),
            jnp.concatenate([-sin, sin, zeros], axis=1))


def _inproj_kernel(x_ref, sc_ref, sh_ref, w_ref, b_ref, gq_ref, gkv_ref, z_ref, h_scr):
    j = pl.program_id(1)

    @pl.when(j == 0)
    def _():
        h_scr[...] = (x_ref[...] * (1.0 + sc_ref[...]) + sh_ref[...]).astype(BF16)

    acc = jnp.dot(h_scr[...], w_ref[...], preferred_element_type=F32) + b_ref[...]

    def rms(g):
        ms = jnp.mean(acc * acc, axis=-1, keepdims=True)
        return acc * lax.rsqrt(ms + RMS_EPS) * g

    @pl.when(j < QA0 // IN_TN)
    def _():
        z_ref[...] = jax.nn.sigmoid(acc).astype(BF16)

    @pl.when(jnp.logical_and(j >= QA0 // IN_TN, j < KA0 // IN_TN))
    def _():
        z_ref[...] = (acc * A_QSCALE).astype(BF16)

    @pl.when(jnp.logical_or(jnp.logical_and(j >= KA0 // IN_TN, j < CQ0 // IN_TN), j == KR0 // IN_TN))
    def _():
        z_ref[...] = acc.astype(BF16)

    @pl.when(j == CQ0 // IN_TN)
    def _():
        z_ref[...] = rms(gq_ref[...]).astype(BF16)

    @pl.when(j == CKV0 // IN_TN)
    def _():
        z_ref[...] = rms(gkv_ref[...]).astype(BF16)


def _in_proj(x2, mod4, w2, b2, gq, gkv, seq):
    t, d = x2.shape
    tm = min(1024, seq)
    tpb = seq // tm
    return pl.pallas_call(
        _inproj_kernel,
        grid=(t // tm, Z_WIDTH // IN_TN),
        in_specs=[pl.BlockSpec((tm, d), lambda i, j: (i, 0)),
                  pl.BlockSpec((None, None, 1, d), lambda i, j: (i // tpb, 1, 0, 0)),
                  pl.BlockSpec((None, None, 1, d), lambda i, j: (i // tpb, 0, 0, 0)),
                  pl.BlockSpec((None, d, IN_TN), lambda i, j: (j, 0, 0)),
                  pl.BlockSpec((1, IN_TN), lambda i, j: (0, j)),
                  pl.BlockSpec((1, Q_LORA), lambda i, j: (0, 0)),
                  pl.BlockSpec((1, KV_LORA), lambda i, j: (0, 0))],
        out_specs=pl.BlockSpec((tm, IN_TN), lambda i, j: (i, j)),
        out_shape=jax.ShapeDtypeStruct((t, Z_WIDTH), BF16),
        scratch_shapes=[pltpu.VMEM((tm, d), BF16)],
        compiler_params=_params("arbitrary", "arbitrary"),
        name="in_proj",
    )(x2, mod4, mod4, w2, b2, gq, gkv)


FLASH_HEADS = 2


def _mla_up_kernel(cq_ref, ckv_ref, kra_ref, krb_ref, cos_ref, sin_ref,
                   wq1_ref, wq2_ref, wk_ref, wv_ref, q_ref, k_ref, v_ref):
    cq = cq_ref[...]
    ckv = ckv_ref[...]
    cos = cos_ref[...]
    sin = sin_ref[...]
    qa = jnp.dot(cq, wq1_ref[...], preferred_element_type=F32)
    qb = jnp.dot(cq, wq2_ref[...], preferred_element_type=F32)
    kn = jnp.dot(ckv, wk_ref[...], preferred_element_type=F32)
    vv = jnp.dot(ckv, wv_ref[...], preferred_element_type=F32).astype(BF16)
    k_rope = (kra_ref[...].astype(F32) * cos + krb_ref[...].astype(F32) * sin).astype(BF16)
    for h in range(B_HEADS):
        c0 = h * B_HEAD_PAD
        n0 = h * QK_NOPE
        grp = h // FLASH_HEADS
        o0 = (h % FLASH_HEADS) * B_HEAD_PAD
        v0 = (h % FLASH_HEADS) * V_HEAD
        q_ref[grp, :, o0:o0 + QK_NOPE] = (qa[:, c0:c0 + QK_NOPE] * B_QSCALE).astype(BF16)
        q_rope = qa[:, c0 + QK_NOPE:c0 + B_HEAD_PAD] * cos + qb[:, n0:n0 + QK_NOPE] * sin
        q_ref[grp, :, o0 + QK_NOPE:o0 + B_HEAD_PAD] = (q_rope * B_QSCALE).astype(BF16)
        k_ref[grp, :, o0:o0 + QK_NOPE] = kn[:, n0:n0 + QK_NOPE].astype(BF16)
        k_ref[grp, :, o0 + QK_NOPE:o0 + B_HEAD_PAD] = k_rope
        v_ref[grp, :, v0:v0 + V_HEAD] = vv[:, n0:n0 + V_HEAD]


def _mla_up(z, cos_t, sin_t, wq1, wq2, wk, wv):
    t = z.shape[0]
    tm = 512
    groups = B_HEADS // FLASH_HEADS
    const = lambda i: (0, 0)
    return pl.pallas_call(
        _mla_up_kernel,
        grid=(t // tm,),
        in_specs=[pl.BlockSpec((tm, Q_LORA), lambda i: (i, CQ0 // Q_LORA)),
                  pl.BlockSpec((tm, KV_LORA), lambda i: (i, CKV0 // KV_LORA)),
                  pl.BlockSpec((tm, LANES), lambda i: (i, KR0 // LANES)),
                  pl.BlockSpec((tm, LANES), lambda i: (i, KR0 // LANES + 1)),
                  pl.BlockSpec((tm, LANES), lambda i: (i, 0)),
                  pl.BlockSpec((tm, LANES), lambda i: (i, 0)),
                  pl.BlockSpec(wq1.shape, const),
                  pl.BlockSpec(wq2.shape, const),
                  pl.BlockSpec(wk.shape, const),
                  pl.BlockSpec(wv.shape, const)],
        out_specs=[pl.BlockSpec((groups, tm, FLASH_HEADS * B_HEAD_PAD), lambda i: (0, i, 0)),
                   pl.BlockSpec((groups, tm, FLASH_HEADS * B_HEAD_PAD), lambda i: (0, i, 0)),
                   pl.BlockSpec((groups, tm, FLASH_HEADS * V_HEAD), lambda i: (0, i, 0))],
        out_shape=[jax.ShapeDtypeStruct((groups, t, FLASH_HEADS * B_HEAD_PAD), BF16),
                   jax.ShapeDtypeStruct((groups, t, FLASH_HEADS * B_HEAD_PAD), BF16),
                   jax.ShapeDtypeStruct((groups, t, FLASH_HEADS * V_HEAD), BF16)],
        compiler_params=_params("arbitrary"),
        name="mla_up",
    )(z, z, z, z, cos_t, sin_t, wq1, wq2, wk, wv)


def _flash_kernel(qi_ref, kj_ref, q_ref, k_ref, v_ref, o_ref, q_s, k_s, v_s, m_scr, l_scr, acc_scr):
    p = pl.program_id(2)
    i = qi_ref[p]
    j = kj_ref[p]
    tq = q_ref.shape[0]
    tk = k_ref.shape[0]

    @pl.when(j == 0)
    def _():
        m_scr[...] = jnp.full(m_scr.shape, -jnp.inf, F32)
        l_scr[...] = jnp.zeros(l_scr.shape, F32)
        acc_scr[...] = jnp.zeros(acc_scr.shape, F32)

    def step(diagonal):
        q_s[...] = q_ref[...]
        k_s[...] = k_ref[...]
        v_s[...] = v_ref[...]
        for hh in range(FLASH_HEADS):
            qk = slice(hh * B_HEAD_PAD, (hh + 1) * B_HEAD_PAD)
            vv = slice(hh * V_HEAD, (hh + 1) * V_HEAD)
            s = _nt_dot(q_s[:, qk], k_s[:, qk])
            if diagonal:
                qpos = lax.broadcasted_iota(jnp.int32, (tq, tk), 0)
                kpos = lax.broadcasted_iota(jnp.int32, (tq, tk), 1)
                s = jnp.where(kpos <= (qpos | (CHUNK - 1)), s, NEG)
            m_prev = m_scr[hh]
            m_new = jnp.maximum(m_prev, jnp.max(s, axis=1, keepdims=True))
            alpha = jnp.exp2(m_prev - m_new)
            pexp = jnp.exp2(s - m_new)
            l_new = alpha * l_scr[hh] + jnp.sum(pexp, axis=1, keepdims=True)
            acc = alpha * acc_scr[hh] + jnp.dot(pexp.astype(BF16), v_s[:, vv],
                                                preferred_element_type=F32)
            m_scr[hh] = m_new
            l_scr[hh] = l_new
            acc_scr[hh] = acc
            if diagonal:
                o_ref[:, vv] = (acc / l_new).astype(BF16)

    @pl.when(j < i)
    def _():
        step(False)

    @pl.when(j == i)
    def _():
        step(True)


def _mla_flash(q, k, v, bsz, seq):
    tq = min(1024, seq)
    nqb = seq // tq
    pairs = [(i, j) for i in range(nqb) for j in range(i + 1)]
    qi = jnp.asarray([p[0] for p in pairs], jnp.int32)
    kj = jnp.asarray([p[1] for p in pairs], jnp.int32)
    qk_w = FLASH_HEADS * B_HEAD_PAD
    v_w = FLASH_HEADS * V_HEAD
    grid_spec = pltpu.PrefetchScalarGridSpec(
        num_scalar_prefetch=2,
        grid=(bsz, B_HEADS // FLASH_HEADS, len(pairs)),
        in_specs=[pl.BlockSpec((None, tq, qk_w), lambda b, h, p, qi, kj: (h, b * nqb + qi[p], 0)),
                  pl.BlockSpec((None, tq, qk_w), lambda b, h, p, qi, kj: (h, b * nqb + kj[p], 0)),
                  pl.BlockSpec((None, tq, v_w), lambda b, h, p, qi, kj: (h, b * nqb + kj[p], 0))],
        out_specs=pl.BlockSpec((tq, v_w), lambda b, h, p, qi, kj: (b * nqb + qi[p], h)),
        scratch_shapes=[pltpu.VMEM((tq, qk_w), BF16), pltpu.VMEM((tq, qk_w), BF16),
                        pltpu.VMEM((tq, v_w), BF16),
                        pltpu.VMEM((FLASH_HEADS, tq, 1), F32), pltpu.VMEM((FLASH_HEADS, tq, 1), F32),
                        pltpu.VMEM((FLASH_HEADS, tq, V_HEAD), F32)],
    )
    return pl.pallas_call(
        _flash_kernel,
        grid_spec=grid_spec,
        out_shape=jax.ShapeDtypeStruct((bsz * seq, B_WIDTH), BF16),
        compiler_params=_params("arbitrary", "arbitrary", "arbitrary"),
        name="mla_flash",
    )(qi, kj, q, k, v)


BAND_TQ = A_LEFT_CHUNKS * CHUNK


BAND_HEADS = 4


def _band_kernel(q_ref, kp_ref, kc_ref, vp_ref, vc_ref, bias_ref, o_ref):
    i = pl.program_id(2)
    first = jnp.where(i == 0, NEG, 0.0).astype(F32)
    outs = []
    for hh in range(BAND_HEADS):
        cols = slice(hh * A_HEAD_DIM, (hh + 1) * A_HEAD_DIM)
        q = q_ref[:, cols]
        s_prev = _nt_dot(q, kp_ref[:, cols]) + bias_ref[hh, :, :BAND_TQ] + first
        s_cur = _nt_dot(q, kc_ref[:, cols]) + bias_ref[hh, :, BAND_TQ:]
        m = jnp.maximum(jnp.max(s_prev, axis=1, keepdims=True), jnp.max(s_cur, axis=1, keepdims=True))
        p_prev = jnp.exp2(s_prev - m)
        p_cur = jnp.exp2(s_cur - m)
        l = jnp.sum(p_prev, axis=1, keepdims=True) + jnp.sum(p_cur, axis=1, keepdims=True)
        o = (jnp.dot(p_prev.astype(BF16), vp_ref[:, cols], preferred_element_type=F32)
             + jnp.dot(p_cur.astype(BF16), vc_ref[:, cols], preferred_element_type=F32))
        outs.append((o / l).astype(BF16))
    for hh in range(BAND_HEADS):
        o_ref[:, hh * A_HEAD_DIM:(hh + 1) * A_HEAD_DIM] = outs[hh]


def _bias_kernel(base_ref, o_ref):
    rows = jnp.broadcast_to(base_ref[...], (BAND_TQ, base_ref.shape[-1]))
    toep = pltpu.roll(rows, 0, 1, stride=1, stride_axis=0)[:, :2 * BAND_TQ]
    qc = lax.broadcasted_iota(jnp.int32, toep.shape, 0) // CHUNK + A_LEFT_CHUNKS
    kc = lax.broadcasted_iota(jnp.int32, toep.shape, 1) // CHUNK
    valid = jnp.logical_and(kc <= qc, kc >= qc - A_LEFT_CHUNKS)
    o_ref[...] = jnp.where(valid, toep * LOG2E, NEG)


def _band_bias(rel_bias):
    period = 3 * BAND_TQ
    heads, n_rel = rel_bias.shape
    rep = lambda col, n: jnp.broadcast_to(rel_bias[:, col:col + 1], (heads, n))
    n_hi = BAND_TQ - REL_CLIP_LEFT
    base = jnp.concatenate([rep(n_rel - 1, n_hi), rel_bias[:, ::-1],
                            rep(0, 2 * BAND_TQ + 1 - n_hi - n_rel), rep(n_rel - 1, BAND_TQ - 1)],
                           axis=1).astype(F32).reshape(heads, 1, period)
    return pl.pallas_call(
        _bias_kernel,
        grid=(heads,),
        in_specs=[pl.BlockSpec((None, 1, period), lambda h: (h, 0, 0))],
        out_specs=pl.BlockSpec((None, BAND_TQ, 2 * BAND_TQ), lambda h: (h, 0, 0)),
        out_shape=jax.ShapeDtypeStruct((heads, BAND_TQ, 2 * BAND_TQ), F32),
        compiler_params=_params("arbitrary"),
        name="band_bias",
    )(base)


def _band_attn(z, bias, bsz, seq):
    nq = seq // BAND_TQ
    width = BAND_HEADS * A_HEAD_DIM
    qb, kb, vb = QA0 // width, KA0 // width, VA0 // width
    cur = lambda c0: (lambda h, b, i: (b * nq + i, c0 + h))
    prev = lambda c0: (lambda h, b, i: (b * nq + jnp.maximum(i - 1, 0), c0 + h))
    blk = (BAND_TQ, width)
    return pl.pallas_call(
        _band_kernel,
        grid=(A_HEADS // BAND_HEADS, bsz, nq),
        in_specs=[pl.BlockSpec(blk, cur(qb)),
                  pl.BlockSpec(blk, prev(kb)), pl.BlockSpec(blk, cur(kb)),
                  pl.BlockSpec(blk, prev(vb)), pl.BlockSpec(blk, cur(vb)),
                  pl.BlockSpec((BAND_HEADS, BAND_TQ, 2 * BAND_TQ), lambda h, b, i: (h, 0, 0))],
        out_specs=pl.BlockSpec(blk, lambda h, b, i: (b * nq + i, h)),
        out_shape=jax.ShapeDtypeStruct((bsz * seq, A_WIDTH), BF16),
        compiler_params=_params("arbitrary", "arbitrary", "arbitrary"),
        name="band_attn",
    )(z, z, z, z, z, bias)


def _outproj_kernel(ya_ref, yb_ref, ga_ref, gb_ref, x_ref, g1_ref, sc2_ref, sh2_ref,
                    wpa_ref, wpb_ref, wo_ref, lg_ref, lb_ref, x1_ref, h2_ref):
    ta = jnp.dot(ya_ref[...], wpa_ref[...], preferred_element_type=F32)
    tb = jnp.dot(yb_ref[...], wpb_ref[...], preferred_element_type=F32)
    u = ga_ref[...].astype(F32) * ta + gb_ref[...].astype(F32) * tb
    y = jnp.dot(u.astype(BF16), wo_ref[...], preferred_element_type=F32)
    x1 = _layer_norm(DN_ALPHA * x_ref[...] + g1_ref[...] * y, lg_ref[...], lb_ref[...])
    x1_ref[...] = x1
    h2_ref[...] = (x1 * (1.0 + sc2_ref[...]) + sh2_ref[...]).astype(BF16)


def _out_proj(ya, yb, z, x2, mod4, wpa, wpb, wo, lg, lb, seq):
    t, d = x2.shape
    tm = 512
    tpb = seq // tm
    const = lambda i: (0, 0)
    resident = lambda w: pl.BlockSpec(w.shape, const, pipeline_mode=pl.Buffered(1))
    modspec = lambda k: pl.BlockSpec((None, None, 1, d), lambda i: (i // tpb, k, 0, 0))
    return pl.pallas_call(
        _outproj_kernel,
        grid=(t // tm,),
        in_specs=[pl.BlockSpec((tm, A_WIDTH), lambda i: (i, 0)),
                  pl.BlockSpec((tm, B_WIDTH), lambda i: (i, 0)),
                  pl.BlockSpec((tm, d), lambda i: (i, GA0 // d)),
                  pl.BlockSpec((tm, d), lambda i: (i, GB0 // d)),
                  pl.BlockSpec((tm, d), lambda i: (i, 0)),
                  modspec(2), modspec(4), modspec(3),
                  resident(wpa), resident(wpb), resident(wo),
                  pl.BlockSpec((1, d), const), pl.BlockSpec((1, d), const)],
        out_specs=[pl.BlockSpec((tm, d), lambda i: (i, 0)),
                   pl.BlockSpec((tm, d), lambda i: (i, 0))],
        out_shape=[jax.ShapeDtypeStruct((t, d), F32), jax.ShapeDtypeStruct((t, d), BF16)],
        compiler_params=_params("arbitrary"),
        name="out_proj",
    )(ya, yb, z, z, x2, mod4, mod4, mod4, wpa, wpb, wo, lg, lb)


TOPV = PEER_TOPK + 1
TOPV_ROWS = 24
SUBLANES_F32 = 8


def _sort_network(n):
    def merge(lo, hi, r):
        step = r * 2
        if step < hi - lo:
            yield from merge(lo, hi, step)
            yield from merge(lo + r, hi, step)
            yield from [(i, i + r) for i in range(lo + r, hi - r, step)]
        else:
            yield (lo, lo + r)

    def sort(lo, hi):
        if hi - lo >= 1:
            mid = lo + (hi - lo) // 2
            yield from sort(lo, mid)
            yield from sort(mid + 1, hi)
            yield from merge(lo, hi, 1)

    return list(sort(0, n - 1))


def _pop_top(lists, singles, out_scr):
    depth = len(lists)
    for r in range(TOPV):
        head = lists[0]
        for x in singles:
            head = jnp.maximum(head, x)
        m = jnp.max(head, axis=0, keepdims=True)
        out_scr[r:r + 1, :] = m
        remaining = TOPV - 1 - r
        if remaining == 0:
            break
        hit = lists[0] == m
        for k in range(min(depth, remaining)):
            nxt = lists[k + 1] if k + 1 < depth else -jnp.inf
            lists[k] = jnp.where(hit, nxt, lists[k])
        singles = [jnp.where(x == m, -jnp.inf, x) for x in singles]


def _top_vals(s, out_scr):
    n = s.shape[0] // SUBLANES_F32
    lists = [s[k * SUBLANES_F32:(k + 1) * SUBLANES_F32, :] for k in range(n)]
    for a, b in _sort_network(n):
        lists[a], lists[b] = jnp.maximum(lists[a], lists[b]), jnp.minimum(lists[a], lists[b])
    _pop_top(lists, [], out_scr)


def _topk_kernel(h2_ref, wqt_ref, keys_ref, tau_ref, e1_ref, s2_ref, e2_ref,
                 st_scr, tv1_scr, tv2_scr, fv_scr):
    pqt = _nt_dot(wqt_ref[...], h2_ref[...]).astype(BF16)
    for hp in range(2 * PEER_HEADS):
        r0 = hp * PEER_HALF
        st_scr[r0:r0 + N_KEYS, :] = jnp.dot(keys_ref[hp], pqt[r0:r0 + PEER_HALF, :],
                                            preferred_element_type=F32)
    pad = jnp.full((TOPV_ROWS - PEER_TOPK, h2_ref.shape[0]), -jnp.inf, F32)
    tv1_scr[PEER_TOPK:, :] = pad
    tv2_scr[PEER_TOPK:, :] = pad

    def head(h, carry):
        r1 = pl.multiple_of(h * 2 * N_KEYS, 2 * N_KEYS)
        s1 = st_scr[pl.ds(r1, N_KEYS), :]
        s2 = st_scr[pl.ds(r1 + N_KEYS, N_KEYS), :]
        _top_vals(s1, tv1_scr)
        _top_vals(s2, tv2_scr)
        tv1_lo = tv1_scr[0:SUBLANES_F32, :]
        lists = [tv1_lo + tv2_scr[j:j + 1, :] for j in range(PEER_TOPK)]
        singles = [tv1_scr[SUBLANES_F32:PEER_TOPK, :] + tv2_scr[0:1, :],
                   tv1_scr[PEER_TOPK:, :] + tv2_scr[0:1, :],
                   tv2_scr[PEER_TOPK:, :] + tv1_scr[0:1, :]]
        _pop_top(lists, singles, fv_scr)
        top = fv_scr[0:1, :]
        kept = fv_scr[0:PEER_TOPK, :]
        theta = 0.5 * (fv_scr[PEER_TOPK - 1:PEER_TOPK, :] + fv_scr[PEER_TOPK:PEER_TOPK + 1, :])
        inv_z = 1.0 / jnp.sum(jnp.exp(kept - top), axis=0, keepdims=True)
        tau_ref[h] = theta - s1
        e1_ref[h] = jnp.exp(s1 - tv1_scr[0:1, :]) * inv_z
        s2_ref[h] = s2
        e2_ref[h] = jnp.exp(s2 - tv2_scr[0:1, :])
        return carry

    lax.fori_loop(0, PEER_HEADS, head, 0)


def _peer_topk(h2, wqt, keys_b):
    t, d = h2.shape
    tm = 512
    rows = PEER_HEADS * 2 * N_KEYS
    out = jax.ShapeDtypeStruct((PEER_HEADS, N_KEYS, t), F32)
    ospec = pl.BlockSpec((PEER_HEADS, N_KEYS, tm), lambda i: (0, 0, i))
    return pl.pallas_call(
        _topk_kernel,
        grid=(t // tm,),
        in_specs=[pl.BlockSpec((tm, d), lambda i: (i, 0)),
                  pl.BlockSpec(wqt.shape, lambda i: (0, 0)),
                  pl.BlockSpec(keys_b.shape, lambda i: (0, 0, 0))],
        out_specs=[ospec] * 4,
        out_shape=[out] * 4,
        scratch_shapes=[pltpu.VMEM((rows, tm), F32),
                        pltpu.VMEM((TOPV_ROWS, tm), F32),
                        pltpu.VMEM((TOPV_ROWS, tm), F32),
                        pltpu.VMEM((TOPV_ROWS, tm), F32)],
        compiler_params=_params("arbitrary"),
        name="peer_topk",
    )(h2, wqt, keys_b)


PEER_TE = 1024
PEER_TM = 512
GATE_K1_TILE = 2


def _dense_stages(u_blk, vt_blk, h2_s, tau_s, e1_s, s2_s, e2_s, acc_s, at_new, at_old, wt_new, wt_old):
    acc_s[...] += jnp.dot(vt_blk[...], wt_old[...], preferred_element_type=F32)
    at_new[...] = _nt_dot(u_blk[...], h2_s[...])
    rows = SUBLANES_BF16
    for lb in range(PEER_TM // LANES):
        lanes = slice(lb * LANES, (lb + 1) * LANES)
        for g in range(N_KEYS // rows):
            krows = slice(g * rows, (g + 1) * rows)
            for k0 in range(0, PEER_TE // N_KEYS, GATE_K1_TILE):
                gates = [jnp.zeros((rows, LANES), F32) for _ in range(GATE_K1_TILE)]
                for h in range(PEER_HEADS):
                    s2 = s2_s[h, krows, lanes]
                    e2 = e2_s[h, krows, lanes]
                    for j in range(GATE_K1_TILE):
                        tau = tau_s[h, k0 + j:k0 + j + 1, lanes]
                        e1 = e1_s[h, k0 + j:k0 + j + 1, lanes]
                        gates[j] = gates[j] + jnp.where(s2 >= tau, e2, 0.0) * e1
                for j in range(GATE_K1_TILE):
                    r0 = (k0 + j) * N_KEYS + g * rows
                    a = at_old[r0:r0 + rows, lanes]
                    w = 0.5 * a * (1.0 + lax.erf(a * SQRT_HALF)) * gates[j]
                    wt_new[r0:r0 + rows, lanes] = w.astype(BF16)


def _dense_kernel(h2_ref, u_hbm, vt_hbm, tau_ref, e1_ref, s2_ref, e2_ref, pt_ref,
                  u_buf, vt_buf, u_sem, vt_sem, at0_scr, at1_scr, wt0_scr, wt1_scr,
                  acc_s, h2_s, tau_s, e1_s, s2_s, e2_s, *, n_e, n_blocks):
    s = pl.program_id(0)
    n_steps = n_blocks + 2
    blk = lambda step, lag: jnp.clip(step - lag, 0, n_blocks - 1)
    e_a = lax.rem(blk(s, 0), n_e)
    e_b = lax.rem(blk(s, 1), n_e)
    e_c = lax.rem(blk(s, 2), n_e)

    def u_copy(step, slot):
        rows = pl.ds(pl.multiple_of(lax.rem(blk(step, 0), n_e) * PEER_TE, PEER_TE), PEER_TE)
        return pltpu.make_async_copy(u_hbm.at[rows, :], u_buf.at[slot], u_sem.at[slot])

    def vt_copy(step, slot):
        return pltpu.make_async_copy(vt_hbm.at[lax.rem(blk(step, 2), n_e)], vt_buf.at[slot],
                                     vt_sem.at[slot])

    @pl.when(s == 0)
    def _():
        for scr in (at0_scr, at1_scr, wt0_scr, wt1_scr):
            scr[...] = jnp.zeros(scr.shape, scr.dtype)
        u_copy(s, 0).start()
        vt_copy(s, 0).start()

    @pl.when(e_a == 0)
    def _():
        h2_s[...] = h2_ref[...]

    @pl.when(e_b == 0)
    def _():
        s2_s[...] = s2_ref[...]
        e2_s[...] = e2_ref[...]

    @pl.when(e_c == 0)
    def _():
        acc_s[...] = jnp.zeros(acc_s.shape, F32)

    def step_body(slot, at_new, at_old, wt_new, wt_old):
        @pl.when(s + 1 < n_steps)
        def _():
            u_copy(s + 1, 1 - slot).start()
            vt_copy(s + 1, 1 - slot).start()

        u_copy(s, slot).wait()
        vt_copy(s, slot).wait()
        tau_s[...] = tau_ref[...]
        e1_s[...] = e1_ref[...]
        _dense_stages(u_buf.at[slot], vt_buf.at[slot], h2_s, tau_s, e1_s, s2_s, e2_s, acc_s,
                      at_new, at_old, wt_new, wt_old)

    @pl.when(lax.rem(s, 2) == 0)
    def _():
        step_body(0, at0_scr, at1_scr, wt1_scr, wt0_scr)

    @pl.when(lax.rem(s, 2) == 1)
    def _():
        step_body(1, at1_scr, at0_scr, wt0_scr, wt1_scr)

    @pl.when(jnp.logical_and(e_c == n_e - 1, s >= 2))
    def _():
        pt_ref[...] = acc_s[...]


def _peer_dense(h2, u_b, vt_b, tau, e1, s2, e2):
    t, d = h2.shape
    n_e = u_b.shape[0] // PEER_TE
    n_blocks = (t // PEER_TM) * n_e
    k1_blk = PEER_TE // N_KEYS
    blk = lambda lag: (lambda s: jnp.clip(s - lag, 0, n_blocks - 1))
    tile = lambda lag: (lambda s: blk(lag)(s) // n_e)
    eblk = lambda lag: (lambda s: lax.rem(blk(lag)(s), n_e))
    rspec = pl.BlockSpec((PEER_HEADS, k1_blk, PEER_TM), lambda s: (0, eblk(1)(s), tile(1)(s)))
    fspec = pl.BlockSpec((PEER_HEADS, N_KEYS, PEER_TM), lambda s: (0, 0, tile(1)(s)),
                         pipeline_mode=pl.Buffered(1))
    any_spec = pl.BlockSpec(memory_space=pl.ANY)
    return pl.pallas_call(
        functools.partial(_dense_kernel, n_e=n_e, n_blocks=n_blocks),
        grid=(n_blocks + 2,),
        in_specs=[pl.BlockSpec((PEER_TM, d), lambda s: (tile(0)(s), 0)),
                  any_spec, any_spec, rspec, rspec, fspec, fspec],
        out_specs=pl.BlockSpec((d, PEER_TM), lambda s: (0, tile(2)(s))),
        out_shape=jax.ShapeDtypeStruct((d, t), F32),
        scratch_shapes=[pltpu.VMEM((2, PEER_TE, d), BF16), pltpu.VMEM((2, d, PEER_TE), BF16),
                        pltpu.SemaphoreType.DMA((2,)), pltpu.SemaphoreType.DMA((2,)),
                        pltpu.VMEM((PEER_TE, PEER_TM), F32), pltpu.VMEM((PEER_TE, PEER_TM), F32),
                        pltpu.VMEM((PEER_TE, PEER_TM), BF16), pltpu.VMEM((PEER_TE, PEER_TM), BF16),
                        pltpu.VMEM((d, PEER_TM), F32), pltpu.VMEM((PEER_TM, d), BF16),
                        pltpu.VMEM((PEER_HEADS, k1_blk, PEER_TM), F32),
                        pltpu.VMEM((PEER_HEADS, k1_blk, PEER_TM), F32),
                        pltpu.VMEM((PEER_HEADS, N_KEYS, PEER_TM), F32),
                        pltpu.VMEM((PEER_HEADS, N_KEYS, PEER_TM), F32)],
        compiler_params=_params("arbitrary"),
        name="peer_dense",
    )(h2, u_b, vt_b, tau, e1, s2, e2)


def _final_kernel(pt_ref, x1_ref, g2_ref, lg_ref, lb_ref, o_ref):
    p = pt_ref[...].T
    o_ref[...] = _layer_norm(DN_ALPHA * x1_ref[...] + g2_ref[...] * p, lg_ref[...], lb_ref[...])


def _peer_final(pt, x1, mod4, lg, lb, seq):
    t, d = x1.shape
    tm = 512
    tpb = seq // tm
    return pl.pallas_call(
        _final_kernel,
        grid=(t // tm,),
        in_specs=[pl.BlockSpec((d, tm), lambda i: (0, i)),
                  pl.BlockSpec((tm, d), lambda i: (i, 0)),
                  pl.BlockSpec((None, None, 1, d), lambda i: (i // tpb, 5, 0, 0)),
                  pl.BlockSpec((1, d), lambda i: (0, 0)),
                  pl.BlockSpec((1, d), lambda i: (0, 0))],
        out_specs=pl.BlockSpec((tm, d), lambda i: (i, 0)),
        out_shape=jax.ShapeDtypeStruct((t, d), F32),
        compiler_params=_params("arbitrary"),
        name="peer_final",
    )(pt, x1, mod4, lg, lb)


def _half_swap(w):
    half = w.shape[-1] // 2
    return jnp.concatenate([w[..., half:], w[..., :half]], axis=-1)


def _in_proj_layout(w, pad_to):
    o = np.cumsum([0, A_WIDTH, A_WIDTH, A_WIDTH, Q_LORA, KV_LORA, QK_ROPE, pad_to, pad_to])
    qa, ka, va, cq, ckv, kr, ga, gb = [w[..., o[n]:o[n + 1]] for n in range(8)]
    z64 = jnp.zeros(kr.shape[:-1] + (LANES - QK_ROPE,), w.dtype)
    ztail = jnp.zeros(kr.shape[:-1] + (IN_TN - 2 * LANES,), w.dtype)
    return jnp.concatenate([ga, gb, qa, ka, va, cq, ckv, kr, z64, _half_swap(kr), z64, ztail], axis=-1)


def _mla_weight_layout(w_uq, w_ukv):
    r = w_uq.shape[0]
    uq = w_uq.reshape(r, B_HEADS, QK_NOPE + QK_ROPE)
    nope, rope = uq[..., :QK_NOPE], uq[..., QK_NOPE:]
    z64 = jnp.zeros((r, B_HEADS, B_HEAD_PAD - QK_NOPE - QK_ROPE), w_uq.dtype)
    wq1 = jnp.concatenate([nope, rope, z64], axis=-1).reshape(r, B_HEADS * B_HEAD_PAD)
    wq2 = jnp.concatenate([_half_swap(rope), z64], axis=-1).reshape(r, B_HEADS * LANES)
    ukv = w_ukv.reshape(w_ukv.shape[0], B_HEADS, QK_NOPE + V_HEAD)
    wk = ukv[..., :QK_NOPE].reshape(-1, B_HEADS * QK_NOPE)
    wv = ukv[..., QK_NOPE:].reshape(-1, B_WIDTH)
    return wq1.astype(BF16), wq2.astype(BF16), wk.astype(BF16), wv.astype(BF16)


def kernel(x, c, positions, w_ada, b_ada, w_in, b_in, rel_bias, q_norm_g, kv_norm_g, w_uq, w_ukv,
           w_pa, w_pb, w_o, ln1_g, ln1_b, peer_wq, peer_keys, peer_u, peer_v, ln2_g, ln2_b):
    bsz, seq, d = x.shape
    t = bsz * seq
    assert d == GB0 - GA0 and seq % 1024 == 0 and w_ada.shape[0] == 1
    x2 = x.reshape(t, d)
    row = lambda a: a.reshape(1, -1).astype(F32)

    mod4 = _ada_mod(c, w_ada[0], b_ada[0]).reshape(bsz, 6, 1, d)
    cos_t, sin_t = _rope_tables(positions)

    w2 = _in_proj_layout(w_in[0], d).astype(BF16).reshape(d, Z_WIDTH // IN_TN, IN_TN).transpose(1, 0, 2)
    b2 = _in_proj_layout(b_in[0], d).reshape(1, Z_WIDTH).astype(F32)
    z = _in_proj(x2, mod4, w2, b2, row(q_norm_g[0]), row(kv_norm_g[0]), seq)

    wq1, wq2, wk, wv = _mla_weight_layout(w_uq[0], w_ukv[0])
    qb, kb, vb = _mla_up(z, cos_t, sin_t, wq1, wq2, wk, wv)
    yb = _mla_flash(qb, kb, vb, bsz, seq)
    ya = _band_attn(z, _band_bias(rel_bias[0]), bsz, seq)

    x1, h2 = _out_proj(ya, yb, z, x2, mod4, w_pa[0].astype(BF16), w_pb[0].astype(BF16),
                       w_o[0].astype(BF16), row(ln1_g[0]), row(ln1_b[0]), seq)

    wqt = peer_wq[0].astype(BF16).T
    keys_b = peer_keys[0].reshape(2 * PEER_HEADS, N_KEYS, PEER_HALF).astype(BF16)
    tau, e1, s2, e2 = _peer_topk(h2, wqt, keys_b)
    vt_b = peer_v[0].astype(BF16).reshape(-1, PEER_TE, d).transpose(0, 2, 1)
    pt = _peer_dense(h2, peer_u[0].astype(BF16), vt_b, tau, e1, s2, e2)
    out = _peer_final(pt, x1, mod4, row(ln2_g[0]), row(ln2_b[0]), seq)
    return out.reshape(bsz, seq, d)
```

```python
import functools
import math

import jax
import jax.numpy as jnp
import numpy as np
from jax import lax
from jax.experimental import pallas as pl
from jax.experimental.pallas import tpu as pltpu

F32 = jnp.float32
BF16 = jnp.bfloat16

CHUNK = 64
A_HEADS = 8
A_HEAD_DIM = 128
A_LEFT_CHUNKS = 8
REL_CLIP_LEFT = 256
REL_CLIP_RIGHT = CHUNK - 1
B_HEADS = 8
Q_LORA = 512
KV_LORA = 512
QK_NOPE = 128
QK_ROPE = 64
V_HEAD = 128
ROPE_THETA = 10000.0
N_KEYS = 128
PEER_HEADS = 8
PEER_TOPK = 16
PEER_HALF = 128
DEPTH = 1
DN_ALPHA = (2 * DEPTH) ** 0.25
LN_EPS = 1e-5
RMS_EPS = 1e-6
NEG = -1e30

A_WIDTH = A_HEADS * A_HEAD_DIM
B_WIDTH = B_HEADS * V_HEAD
LOG2E = math.log2(math.e)
A_QSCALE = A_HEAD_DIM ** -0.5 * LOG2E
B_QSCALE = (QK_NOPE + QK_ROPE) ** -0.5 * LOG2E
SQRT_HALF = math.sqrt(0.5)

LANES = 128
SUBLANES_BF16 = 16
VMEM_LIMIT_BYTES = 56 * 1024 * 1024

IN_TN = 512
GA0 = 0
GB0 = 2048
QA0 = 4096
KA0 = 5120
VA0 = 6144
CQ0 = 7168
CKV0 = 7680
KR0 = 8192
Z_WIDTH = 8704
B_HEAD_PAD = 256


def _nt_dot(a, b):
    return lax.dot_general(a, b, (((1,), (1,)), ((), ())), preferred_element_type=F32)


def _params(*sem, flags=None, vmem=VMEM_LIMIT_BYTES):
    return pltpu.CompilerParams(dimension_semantics=sem, vmem_limit_bytes=vmem, flags=flags)


def _layer_norm(r, g, b):
    mu = jnp.mean(r, axis=-1, keepdims=True)
    d = r - mu
    var = jnp.mean(d * d, axis=-1, keepdims=True)
    return d * lax.rsqrt(var + LN_EPS) * g + b


def _ada_kernel(c_ref, w_ref, b_ref, o_ref):
    c = c_ref[...]
    cond = (c * jax.nn.sigmoid(c)).astype(BF16)
    o_ref[...] = jnp.dot(cond, w_ref[...].astype(BF16), preferred_element_type=F32) + b_ref[...]


def _ada_mod(c, w_ada, b_ada):
    bsz, d = c.shape
    n = w_ada.shape[1]
    tn = 1024
    return pl.pallas_call(
        _ada_kernel,
        grid=(n // tn,),
        in_specs=[pl.BlockSpec((bsz, d), lambda j: (0, 0)),
                  pl.BlockSpec((d, tn), lambda j: (0, j)),
                  pl.BlockSpec((1, tn), lambda j: (0, j))],
        out_specs=pl.BlockSpec((bsz, tn), lambda j: (0, j)),
        out_shape=jax.ShapeDtypeStruct((bsz, n), F32),
        compiler_params=_params("arbitrary"),
        name="ada_mod",
    )(c, w_ada, b_ada.reshape(1, n))


def _rope_kernel(pos_ref, invf_ref, cos_ref, sin_ref):
    ang = pos_ref[...] * invf_ref[...]
    cos_ref[...] = jnp.cos(ang)
    sin_ref[...] = jnp.sin(ang)


def _rope_tables(positions):
    t = positions.size
    half = QK_ROPE // 2
    per_row = LANES // half
    rows = t // per_row
    inv_freq = ROPE_THETA ** (-jnp.arange(0, QK_ROPE, 2, dtype=F32) / QK_ROPE)
    pos_rep = jnp.repeat(positions.astype(F32).reshape(rows, per_row), half, axis=1)
    invf = jnp.tile(inv_freq, per_row).reshape(1, LANES)
    tr = min(rows, 2048)
    cos, sin = pl.pallas_call(
        _rope_kernel,
        grid=(rows // tr,),
        in_specs=[pl.BlockSpec((tr, LANES), lambda i: (i, 0)),
                  pl.BlockSpec((1, LANES), lambda i: (0, 0))],
        out_specs=[pl.BlockSpec((tr, LANES), lambda i: (i, 0))] * 2,
        out_shape=[jax.ShapeDtypeStruct((rows, LANES), F32)] * 2,
        compiler_params=_params("arbitrary"),
        name="rope_tab",
    )(pos_rep, invf)
    cos = cos.reshape(t, half)
    sin = sin.reshape(t, half)
    zeros = jnp.zeros((t, LANES - QK_ROPE), F32)
    return (jnp.concatenate([cos, cos, zeros], axis=1),
            jnp.concatenate([-sin, sin, zeros], axis=1))


def _inproj_kernel(x_ref, sc_ref, sh_ref, w_ref, b_ref, gq_ref, gkv_ref, z_ref, h_scr):
    j = pl.program_id(1)

    @pl.when(j == 0)
    def _():
        h_scr[...] = (x_ref[...] * (1.0 + sc_ref[...]) + sh_ref[...]).astype(BF16)

    acc = jnp.dot(h_scr[...], w_ref[...], preferred_element_type=F32) + b_ref[...]

    def rms(g):
        ms = jnp.mean(acc * acc, axis=-1, keepdims=True)
        return acc * lax.rsqrt(ms + RMS_EPS) * g

    @pl.when(j < QA0 // IN_TN)
    def _():
        z_ref[...] = jax.nn.sigmoid(acc).astype(BF16)

    @pl.when(jnp.logical_and(j >= QA0 // IN_TN, j < KA0 // IN_TN))
    def _():
        z_ref[...] = (acc * A_QSCALE).astype(BF16)

    @pl.when(jnp.logical_or(jnp.logical_and(j >= KA0 // IN_TN, j < CQ0 // IN_TN), j == KR0 // IN_TN))
    def _():
        z_ref[...] = acc.astype(BF16)

    @pl.when(j == CQ0 // IN_TN)
    def _():
        z_ref[...] = rms(gq_ref[...]).astype(BF16)

    @pl.when(j == CKV0 // IN_TN)
    def _():
        z_ref[...] = rms(gkv_ref[...]).astype(BF16)


def _in_proj(x2, mod4, w2, b2, gq, gkv, seq):
    t, d = x2.shape
    tm = min(1024, seq)
    tpb = seq // tm
    return pl.pallas_call(
        _inproj_kernel,
        grid=(t // tm, Z_WIDTH // IN_TN),
        in_specs=[pl.BlockSpec((tm, d), lambda i, j: (i, 0)),
                  pl.BlockSpec((None, None, 1, d), lambda i, j: (i // tpb, 1, 0, 0)),
                  pl.BlockSpec((None, None, 1, d), lambda i, j: (i // tpb, 0, 0, 0)),
                  pl.BlockSpec((None, d, IN_TN), lambda i, j: (j, 0, 0)),
                  pl.BlockSpec((1, IN_TN), lambda i, j: (0, j)),
                  pl.BlockSpec((1, Q_LORA), lambda i, j: (0, 0)),
                  pl.BlockSpec((1, KV_LORA), lambda i, j: (0, 0))],
        out_specs=pl.BlockSpec((tm, IN_TN), lambda i, j: (i, j)),
        out_shape=jax.ShapeDtypeStruct((t, Z_WIDTH), BF16),
        scratch_shapes=[pltpu.VMEM((tm, d), BF16)],
        compiler_params=_params("arbitrary", "arbitrary"),
        name="in_proj",
    )(x2, mod4, mod4, w2, b2, gq, gkv)


FLASH_HEADS = 4


def _mla_up_kernel(cq_ref, ckv_ref, kra_ref, krb_ref, cos_ref, sin_ref,
                   wq1_ref, wq2_ref, wk_ref, wv_ref, q_ref, k_ref, v_ref):
    cq = cq_ref[...]
    ckv = ckv_ref[...]
    cos = cos_ref[...]
    sin = sin_ref[...]
    qa = jnp.dot(cq, wq1_ref[...], preferred_element_type=F32)
    qb = jnp.dot(cq, wq2_ref[...], preferred_element_type=F32)
    kn = jnp.dot(ckv, wk_ref[...], preferred_element_type=F32)
    vv = jnp.dot(ckv, wv_ref[...], preferred_element_type=F32).astype(BF16)
    k_rope = (kra_ref[...].astype(F32) * cos + krb_ref[...].astype(F32) * sin).astype(BF16)
    for h in range(B_HEADS):
        c0 = h * B_HEAD_PAD
        n0 = h * QK_NOPE
        grp = h // FLASH_HEADS
        o0 = (h % FLASH_HEADS) * B_HEAD_PAD
        v0 = (h % FLASH_HEADS) * V_HEAD
        q_ref[grp, :, o0:o0 + QK_NOPE] = (qa[:, c0:c0 + QK_NOPE] * B_QSCALE).astype(BF16)
        q_rope = qa[:, c0 + QK_NOPE:c0 + B_HEAD_PAD] * cos + qb[:, n0:n0 + QK_NOPE] * sin
        q_ref[grp, :, o0 + QK_NOPE:o0 + B_HEAD_PAD] = (q_rope * B_QSCALE).astype(BF16)
        k_ref[grp, :, o0:o0 + QK_NOPE] = kn[:, n0:n0 + QK_NOPE].astype(BF16)
        k_ref[grp, :, o0 + QK_NOPE:o0 + B_HEAD_PAD] = k_rope
        v_ref[grp, :, v0:v0 + V_HEAD] = vv[:, n0:n0 + V_HEAD]


def _mla_up(z, cos_t, sin_t, wq1, wq2, wk, wv):
    t = z.shape[0]
    tm = 512
    groups = B_HEADS // FLASH_HEADS
    const = lambda i: (0, 0)
    return pl.pallas_call(
        _mla_up_kernel,
        grid=(t // tm,),
        in_specs=[pl.BlockSpec((tm, Q_LORA), lambda i: (i, CQ0 // Q_LORA)),
                  pl.BlockSpec((tm, KV_LORA), lambda i: (i, CKV0 // KV_LORA)),
                  pl.BlockSpec((tm, LANES), lambda i: (i, KR0 // LANES)),
                  pl.BlockSpec((tm, LANES), lambda i: (i, KR0 // LANES + 1)),
                  pl.BlockSpec((tm, LANES), lambda i: (i, 0)),
                  pl.BlockSpec((tm, LANES), lambda i: (i, 0)),
                  pl.BlockSpec(wq1.shape, const),
                  pl.BlockSpec(wq2.shape, const),
                  pl.BlockSpec(wk.shape, const),
                  pl.BlockSpec(wv.shape, const)],
        out_specs=[pl.BlockSpec((groups, tm, FLASH_HEADS * B_HEAD_PAD), lambda i: (0, i, 0)),
                   pl.BlockSpec((groups, tm, FLASH_HEADS * B_HEAD_PAD), lambda i: (0, i, 0)),
                   pl.BlockSpec((groups, tm, FLASH_HEADS * V_HEAD), lambda i: (0, i, 0))],
        out_shape=[jax.ShapeDtypeStruct((groups, t, FLASH_HEADS * B_HEAD_PAD), BF16),
                   jax.ShapeDtypeStruct((groups, t, FLASH_HEADS * B_HEAD_PAD), BF16),
                   jax.ShapeDtypeStruct((groups, t, FLASH_HEADS * V_HEAD), BF16)],
        compiler_params=_params("arbitrary"),
        name="mla_up",
    )(z, z, z, z, cos_t, sin_t, wq1, wq2, wk, wv)


def _flash_kernel(qi_ref, kj_ref, q_ref, k_ref, v_ref, o_ref, q_s, k_s, v_s, m_scr, l_scr, acc_scr):
    p = pl.program_id(2)
    i = qi_ref[p]
    j = kj_ref[p]
    tq = q_ref.shape[0]
    tk = k_ref.shape[0]

    @pl.when(j == 0)
    def _():
        m_scr[...] = jnp.full(m_scr.shape, -jnp.inf, F32)
        l_scr[...] = jnp.zeros(l_scr.shape, F32)
        acc_scr[...] = jnp.zeros(acc_scr.shape, F32)

    def step(diagonal):
        q_s[...] = q_ref[...]
        k_s[...] = k_ref[...]
        v_s[...] = v_ref[...]
        for hh in range(FLASH_HEADS):
            qk = slice(hh * B_HEAD_PAD, (hh + 1) * B_HEAD_PAD)
            vv = slice(hh * V_HEAD, (hh + 1) * V_HEAD)
            s = _nt_dot(q_s[:, qk], k_s[:, qk])
            if diagonal:
                qpos = lax.broadcasted_iota(jnp.int32, (tq, tk), 0)
                kpos = lax.broadcasted_iota(jnp.int32, (tq, tk), 1)
                s = jnp.where(kpos <= (qpos | (CHUNK - 1)), s, NEG)
            m_prev = m_scr[hh]
            m_new = jnp.maximum(m_prev, jnp.max(s, axis=1, keepdims=True))
            alpha = jnp.exp2(m_prev - m_new)
            pexp = jnp.exp2(s - m_new)
            l_new = alpha * l_scr[hh] + jnp.sum(pexp, axis=1, keepdims=True)
            acc = alpha * acc_scr[hh] + jnp.dot(pexp.astype(BF16), v_s[:, vv],
                                                preferred_element_type=F32)
            m_scr[hh] = m_new
            l_scr[hh] = l_new
            acc_scr[hh] = acc
            if diagonal:
                o_ref[:, vv] = (acc / l_new).astype(BF16)

    @pl.when(j < i)
    def _():
        step(False)

    @pl.when(j == i)
    def _():
        step(True)


def _mla_flash(q, k, v, bsz, seq):
    tq = min(1024, seq)
    nqb = seq // tq
    pairs = [(i, j) for i in range(nqb) for j in range(i + 1)]
    qi = jnp.asarray([p[0] for p in pairs], jnp.int32)
    kj = jnp.asarray([p[1] for p in pairs], jnp.int32)
    qk_w = FLASH_HEADS * B_HEAD_PAD
    v_w = FLASH_HEADS * V_HEAD
    grid_spec = pltpu.PrefetchScalarGridSpec(
        num_scalar_prefetch=2,
        grid=(bsz, B_HEADS // FLASH_HEADS, len(pairs)),
        in_specs=[pl.BlockSpec((None, tq, qk_w), lambda b, h, p, qi, kj: (h, b * nqb + qi[p], 0)),
                  pl.BlockSpec((None, tq, qk_w), lambda b, h, p, qi, kj: (h, b * nqb + kj[p], 0)),
                  pl.BlockSpec((None, tq, v_w), lambda b, h, p, qi, kj: (h, b * nqb + kj[p], 0))],
        out_specs=pl.BlockSpec((tq, v_w), lambda b, h, p, qi, kj: (b * nqb + qi[p], h)),
        scratch_shapes=[pltpu.VMEM((tq, qk_w), BF16), pltpu.VMEM((tq, qk_w), BF16),
                        pltpu.VMEM((tq, v_w), BF16),
                        pltpu.VMEM((FLASH_HEADS, tq, 1), F32), pltpu.VMEM((FLASH_HEADS, tq, 1), F32),
                        pltpu.VMEM((FLASH_HEADS, tq, V_HEAD), F32)],
    )
    return pl.pallas_call(
        _flash_kernel,
        grid_spec=grid_spec,
        out_shape=jax.ShapeDtypeStruct((bsz * seq, B_WIDTH), BF16),
        compiler_params=_params("arbitrary", "arbitrary", "arbitrary"),
        name="mla_flash",
    )(qi, kj, q, k, v)


BAND_TQ = A_LEFT_CHUNKS * CHUNK


BAND_HEADS = 8


def _band_kernel(q_ref, kp_ref, kc_ref, vp_ref, vc_ref, bias_ref, o_ref):
    i = pl.program_id(2)
    first = jnp.where(i == 0, NEG, 0.0).astype(F32)
    outs = []
    for hh in range(BAND_HEADS):
        cols = slice(hh * A_HEAD_DIM, (hh + 1) * A_HEAD_DIM)
        q = q_ref[:, cols]
        s_prev = _nt_dot(q, kp_ref[:, cols]) + bias_ref[hh, :, :BAND_TQ] + first
        s_cur = _nt_dot(q, kc_ref[:, cols]) + bias_ref[hh, :, BAND_TQ:]
        m = jnp.maximum(jnp.max(s_prev, axis=1, keepdims=True), jnp.max(s_cur, axis=1, keepdims=True))
        p_prev = jnp.exp2(s_prev - m)
        p_cur = jnp.exp2(s_cur - m)
        l = jnp.sum(p_prev, axis=1, keepdims=True) + jnp.sum(p_cur, axis=1, keepdims=True)
        o = (jnp.dot(p_prev.astype(BF16), vp_ref[:, cols], preferred_element_type=F32)
             + jnp.dot(p_cur.astype(BF16), vc_ref[:, cols], preferred_element_type=F32))
        outs.append((o / l).astype(BF16))
    for hh in range(BAND_HEADS):
        o_ref[:, hh * A_HEAD_DIM:(hh + 1) * A_HEAD_DIM] = outs[hh]


def _bias_kernel(base_ref, o_ref):
    rows = jnp.broadcast_to(base_ref[...], (BAND_TQ, base_ref.shape[-1]))
    toep = pltpu.roll(rows, 0, 1, stride=1, stride_axis=0)[:, :2 * BAND_TQ]
    qc = lax.broadcasted_iota(jnp.int32, toep.shape, 0) // CHUNK + A_LEFT_CHUNKS
    kc = lax.broadcasted_iota(jnp.int32, toep.shape, 1) // CHUNK
    valid = jnp.logical_and(kc <= qc, kc >= qc - A_LEFT_CHUNKS)
    o_ref[...] = jnp.where(valid, toep * LOG2E, NEG)


def _band_bias(rel_bias):
    period = 3 * BAND_TQ
    heads, n_rel = rel_bias.shape
    rep = lambda col, n: jnp.broadcast_to(rel_bias[:, col:col + 1], (heads, n))
    n_hi = BAND_TQ - REL_CLIP_LEFT
    base = jnp.concatenate([rep(n_rel - 1, n_hi), rel_bias[:, ::-1],
                            rep(0, 2 * BAND_TQ + 1 - n_hi - n_rel), rep(n_rel - 1, BAND_TQ - 1)],
                           axis=1).astype(F32).reshape(heads, 1, period)
    return pl.pallas_call(
        _bias_kernel,
        grid=(heads,),
        in_specs=[pl.BlockSpec((None, 1, period), lambda h: (h, 0, 0))],
        out_specs=pl.BlockSpec((None, BAND_TQ, 2 * BAND_TQ), lambda h: (h, 0, 0)),
        out_shape=jax.ShapeDtypeStruct((heads, BAND_TQ, 2 * BAND_TQ), F32),
        compiler_params=_params("arbitrary"),
        name="band_bias",
    )(base)


def _band_attn(z, bias, bsz, seq):
    nq = seq // BAND_TQ
    width = BAND_HEADS * A_HEAD_DIM
    qb, kb, vb = QA0 // width, KA0 // width, VA0 // width
    cur = lambda c0: (lambda h, b, i: (b * nq + i, c0 + h))
    prev = lambda c0: (lambda h, b, i: (b * nq + jnp.maximum(i - 1, 0), c0 + h))
    blk = (BAND_TQ, width)
    return pl.pallas_call(
        _band_kernel,
        grid=(A_HEADS // BAND_HEADS, bsz, nq),
        in_specs=[pl.BlockSpec(blk, cur(qb)),
                  pl.BlockSpec(blk, prev(kb)), pl.BlockSpec(blk, cur(kb)),
                  pl.BlockSpec(blk, prev(vb)), pl.BlockSpec(blk, cur(vb)),
                  pl.BlockSpec((BAND_HEADS, BAND_TQ, 2 * BAND_TQ), lambda h, b, i: (h, 0, 0),
                               pipeline_mode=pl.Buffered(1))],
        out_specs=pl.BlockSpec(blk, lambda h, b, i: (b * nq + i, h)),
        out_shape=jax.ShapeDtypeStruct((bsz * seq, A_WIDTH), BF16),
        compiler_params=_params("arbitrary", "arbitrary", "arbitrary"),
        name="band_attn",
    )(z, z, z, z, z, bias)


def _outproj_kernel(ya_ref, yb_ref, ga_ref, gb_ref, x_ref, g1_ref, sc2_ref, sh2_ref,
                    wpa_ref, wpb_ref, wo_ref, lg_ref, lb_ref, x1_ref, h2_ref):
    ta = jnp.dot(ya_ref[...], wpa_ref[...], preferred_element_type=F32)
    tb = jnp.dot(yb_ref[...], wpb_ref[...], preferred_element_type=F32)
    u = ga_ref[...].astype(F32) * ta + gb_ref[...].astype(F32) * tb
    y = jnp.dot(u.astype(BF16), wo_ref[...], preferred_element_type=F32)
    x1 = _layer_norm(DN_ALPHA * x_ref[...] + g1_ref[...] * y, lg_ref[...], lb_ref[...])
    x1_ref[...] = x1
    h2_ref[...] = (x1 * (1.0 + sc2_ref[...]) + sh2_ref[...]).astype(BF16)


def _out_proj(ya, yb, z, x2, mod4, wpa, wpb, wo, lg, lb, seq):
    t, d = x2.shape
    tm = 512
    tpb = seq // tm
    const = lambda i: (0, 0)
    resident = lambda w: pl.BlockSpec(w.shape, const, pipeline_mode=pl.Buffered(1))
    modspec = lambda k: pl.BlockSpec((None, None, 1, d), lambda i: (i // tpb, k, 0, 0))
    return pl.pallas_call(
        _outproj_kernel,
        grid=(t // tm,),
        in_specs=[pl.BlockSpec((tm, A_WIDTH), lambda i: (i, 0)),
                  pl.BlockSpec((tm, B_WIDTH), lambda i: (i, 0)),
                  pl.BlockSpec((tm, d), lambda i: (i, GA0 // d)),
                  pl.BlockSpec((tm, d), lambda i: (i, GB0 // d)),
                  pl.BlockSpec((tm, d), lambda i: (i, 0)),
                  modspec(2), modspec(4), modspec(3),
                  resident(wpa), resident(wpb), resident(wo),
                  pl.BlockSpec((1, d), const), pl.BlockSpec((1, d), const)],
        out_specs=[pl.BlockSpec((tm, d), lambda i: (i, 0)),
                   pl.BlockSpec((tm, d), lambda i: (i, 0))],
        out_shape=[jax.ShapeDtypeStruct((t, d), F32), jax.ShapeDtypeStruct((t, d), BF16)],
        compiler_params=_params("arbitrary"),
        name="out_proj",
    )(ya, yb, z, z, x2, mod4, mod4, mod4, wpa, wpb, wo, lg, lb)


TOPV = PEER_TOPK + 1
TOPV_ROWS = 24
SUBLANES_F32 = 8


def _sort_network(n):
    def merge(lo, hi, r):
        step = r * 2
        if step < hi - lo:
            yield from merge(lo, hi, step)
            yield from merge(lo + r, hi, step)
            yield from [(i, i + r) for i in range(lo + r, hi - r, step)]
        else:
            yield (lo, lo + r)

    def sort(lo, hi):
        if hi - lo >= 1:
            mid = lo + (hi - lo) // 2
            yield from sort(lo, mid)
            yield from sort(mid + 1, hi)
            yield from merge(lo, hi, 1)

    return list(sort(0, n - 1))


def _pop_top(lists, singles, out_scr):
    depth = len(lists)
    for r in range(TOPV):
        head = lists[0]
        for x in singles:
            head = jnp.maximum(head, x)
        m = jnp.max(head, axis=0, keepdims=True)
        out_scr[r:r + 1, :] = m
        remaining = TOPV - 1 - r
        if remaining == 0:
            break
        hit = lists[0] == m
        for k in range(min(depth, remaining)):
            nxt = lists[k + 1] if k + 1 < depth else -jnp.inf
            lists[k] = jnp.where(hit, nxt, lists[k])
        singles = [jnp.where(x == m, -jnp.inf, x) for x in singles]


def _top_vals(s, out_scr):
    n = s.shape[0] // SUBLANES_F32
    lists = [s[k * SUBLANES_F32:(k + 1) * SUBLANES_F32, :] for k in range(n)]
    for a, b in _sort_network(n):
        lists[a], lists[b] = jnp.maximum(lists[a], lists[b]), jnp.minimum(lists[a], lists[b])
    _pop_top(lists, [], out_scr)


def _topk_kernel(h2_ref, wqt_ref, keys_ref, tau_ref, e1_ref, s2_ref, e2_ref,
                 st_scr, tv1_scr, tv2_scr, fv_scr):
    pqt = _nt_dot(wqt_ref[...], h2_ref[...]).astype(BF16)
    for hp in range(2 * PEER_HEADS):
        r0 = hp * PEER_HALF
        st_scr[r0:r0 + N_KEYS, :] = jnp.dot(keys_ref[hp], pqt[r0:r0 + PEER_HALF, :],
                                            preferred_element_type=F32)
    pad = jnp.full((TOPV_ROWS - PEER_TOPK, h2_ref.shape[0]), -jnp.inf, F32)
    tv1_scr[PEER_TOPK:, :] = pad
    tv2_scr[PEER_TOPK:, :] = pad

    def head(h, carry):
        r1 = pl.multiple_of(h * 2 * N_KEYS, 2 * N_KEYS)
        s1 = st_scr[pl.ds(r1, N_KEYS), :]
        s2 = st_scr[pl.ds(r1 + N_KEYS, N_KEYS), :]
        _top_vals(s1, tv1_scr)
        _top_vals(s2, tv2_scr)
        tv1_lo = tv1_scr[0:SUBLANES_F32, :]
        lists = [tv1_lo + tv2_scr[j:j + 1, :] for j in range(PEER_TOPK)]
        singles = [tv1_scr[SUBLANES_F32:PEER_TOPK, :] + tv2_scr[0:1, :],
                   tv1_scr[PEER_TOPK:, :] + tv2_scr[0:1, :],
                   tv2_scr[PEER_TOPK:, :] + tv1_scr[0:1, :]]
        _pop_top(lists, singles, fv_scr)
        top = fv_scr[0:1, :]
        kept = fv_scr[0:PEER_TOPK, :]
        theta = 0.5 * (fv_scr[PEER_TOPK - 1:PEER_TOPK, :] + fv_scr[PEER_TOPK:PEER_TOPK + 1, :])
        inv_z = 1.0 / jnp.sum(jnp.exp(kept - top), axis=0, keepdims=True)
        tau_ref[h] = theta - s1
        e1_ref[h] = jnp.exp(s1 - tv1_scr[0:1, :]) * inv_z
        s2_ref[h] = s2
        e2_ref[h] = jnp.exp(s2 - tv2_scr[0:1, :])
        return carry

    lax.fori_loop(0, PEER_HEADS, head, 0)


def _peer_topk(h2, wqt, keys_b):
    t, d = h2.shape
    tm = 512
    rows = PEER_HEADS * 2 * N_KEYS
    out = jax.ShapeDtypeStruct((PEER_HEADS, N_KEYS, t), F32)
    ospec = pl.BlockSpec((PEER_HEADS, N_KEYS, tm), lambda i: (0, 0, i))
    return pl.pallas_call(
        _topk_kernel,
        grid=(t // tm,),
        in_specs=[pl.BlockSpec((tm, d), lambda i: (i, 0)),
                  pl.BlockSpec(wqt.shape, lambda i: (0, 0)),
                  pl.BlockSpec(keys_b.shape, lambda i: (0, 0, 0))],
        out_specs=[ospec] * 4,
        out_shape=[out] * 4,
        scratch_shapes=[pltpu.VMEM((rows, tm), F32),
                        pltpu.VMEM((TOPV_ROWS, tm), F32),
                        pltpu.VMEM((TOPV_ROWS, tm), F32),
                        pltpu.VMEM((TOPV_ROWS, tm), F32)],
        compiler_params=_params("arbitrary"),
        name="peer_topk",
    )(h2, wqt, keys_b)


PEER_TE = 1024
PEER_TM = 512
GATE_K1_TILE = 2


def _dense_stages(u_blk, vt_blk, h2_s, tau_s, e1_s, s2_s, e2_s, acc_s, at_new, at_old, wt_new, wt_old):
    acc_s[...] += jnp.dot(vt_blk[...], wt_old[...], preferred_element_type=F32)
    at_new[...] = _nt_dot(u_blk[...], h2_s[...])
    rows = SUBLANES_BF16
    for lb in range(PEER_TM // LANES):
        lanes = slice(lb * LANES, (lb + 1) * LANES)
        for g in range(N_KEYS // rows):
            krows = slice(g * rows, (g + 1) * rows)
            for k0 in range(0, PEER_TE // N_KEYS, GATE_K1_TILE):
                gates = [jnp.zeros((rows, LANES), F32) for _ in range(GATE_K1_TILE)]
                for h in range(PEER_HEADS):
                    s2 = s2_s[h, krows, lanes]
                    e2 = e2_s[h, krows, lanes]
                    for j in range(GATE_K1_TILE):
                        tau = tau_s[h, k0 + j:k0 + j + 1, lanes]
                        e1 = e1_s[h, k0 + j:k0 + j + 1, lanes]
                        gates[j] = gates[j] + jnp.where(s2 >= tau, e2, 0.0) * e1
                for j in range(GATE_K1_TILE):
                    r0 = (k0 + j) * N_KEYS + g * rows
                    a = at_old[r0:r0 + rows, lanes]
                    w = 0.5 * a * (1.0 + lax.erf(a * SQRT_HALF)) * gates[j]
                    wt_new[r0:r0 + rows, lanes] = w.astype(BF16)


def _dense_kernel(h2_ref, u_hbm, vt_hbm, tau_ref, e1_ref, s2_ref, e2_ref, pt_ref,
                  u_buf, vt_buf, u_sem, vt_sem, at0_scr, at1_scr, wt0_scr, wt1_scr,
                  acc_s, h2_s, tau_s, e1_s, s2_s, e2_s, *, n_e, n_blocks):
    s = pl.program_id(0)
    n_steps = n_blocks + 2
    blk = lambda step, lag: jnp.clip(step - lag, 0, n_blocks - 1)
    e_a = lax.rem(blk(s, 0), n_e)
    e_b = lax.rem(blk(s, 1), n_e)
    e_c = lax.rem(blk(s, 2), n_e)

    def u_copy(step, slot):
        rows = pl.ds(pl.multiple_of(lax.rem(blk(step, 0), n_e) * PEER_TE, PEER_TE), PEER_TE)
        return pltpu.make_async_copy(u_hbm.at[rows, :], u_buf.at[slot], u_sem.at[slot])

    def vt_copy(step, slot):
        return pltpu.make_async_copy(vt_hbm.at[lax.rem(blk(step, 2), n_e)], vt_buf.at[slot],
                                     vt_sem.at[slot])

    @pl.when(s == 0)
    def _():
        for scr in (at0_scr, at1_scr, wt0_scr, wt1_scr):
            scr[...] = jnp.zeros(scr.shape, scr.dtype)
        u_copy(s, 0).start()
        vt_copy(s, 0).start()

    @pl.when(e_a == 0)
    def _():
        h2_s[...] = h2_ref[...]

    @pl.when(e_b == 0)
    def _():
        s2_s[...] = s2_ref[...]
        e2_s[...] = e2_ref[...]

    @pl.when(e_c == 0)
    def _():
        acc_s[...] = jnp.zeros(acc_s.shape, F32)

    def step_body(slot, at_new, at_old, wt_new, wt_old):
        @pl.when(s + 1 < n_steps)
        def _():
            u_copy(s + 1, 1 - slot).start()
            vt_copy(s + 1, 1 - slot).start()

        u_copy(s, slot).wait()
        vt_copy(s, slot).wait()
        tau_s[...] = tau_ref[...]
        e1_s[...] = e1_ref[...]
        _dense_stages(u_buf.at[slot], vt_buf.at[slot], h2_s, tau_s, e1_s, s2_s, e2_s, acc_s,
                      at_new, at_old, wt_new, wt_old)

    @pl.when(lax.rem(s, 2) == 0)
    def _():
        step_body(0, at0_scr, at1_scr, wt1_scr, wt0_scr)

    @pl.when(lax.rem(s, 2) == 1)
    def _():
        step_body(1, at1_scr, at0_scr, wt0_scr, wt1_scr)

    @pl.when(jnp.logical_and(e_c == n_e - 1, s >= 2))
    def _():
        pt_ref[...] = acc_s[...]


def _peer_dense(h2, u_b, vt_b, tau, e1, s2, e2):
    t, d = h2.shape
    n_e = u_b.shape[0] // PEER_TE
    n_blocks = (t // PEER_TM) * n_e
    k1_blk = PEER_TE // N_KEYS
    blk = lambda lag: (lambda s: jnp.clip(s - lag, 0, n_blocks - 1))
    tile = lambda lag: (lambda s: blk(lag)(s) // n_e)
    eblk = lambda lag: (lambda s: lax.rem(blk(lag)(s), n_e))
    rspec = pl.BlockSpec((PEER_HEADS, k1_blk, PEER_TM), lambda s: (0, eblk(1)(s), tile(1)(s)))
    fspec = pl.BlockSpec((PEER_HEADS, N_KEYS, PEER_TM), lambda s: (0, 0, tile(1)(s)),
                         pipeline_mode=pl.Buffered(1))
    any_spec = pl.BlockSpec(memory_space=pl.ANY)
    return pl.pallas_call(
        functools.partial(_dense_kernel, n_e=n_e, n_blocks=n_blocks),
        grid=(n_blocks + 2,),
        in_specs=[pl.BlockSpec((PEER_TM, d), lambda s: (tile(0)(s), 0)),
                  any_spec, any_spec, rspec, rspec, fspec, fspec],
        out_specs=pl.BlockSpec((d, PEER_TM), lambda s: (0, tile(2)(s))),
        out_shape=jax.ShapeDtypeStruct((d, t), F32),
        scratch_shapes=[pltpu.VMEM((2, PEER_TE, d), BF16), pltpu.VMEM((2, d, PEER_TE), BF16),
                        pltpu.SemaphoreType.DMA((2,)), pltpu.SemaphoreType.DMA((2,)),
                        pltpu.VMEM((PEER_TE, PEER_TM), F32), pltpu.VMEM((PEER_TE, PEER_TM), F32),
                        pltpu.VMEM((PEER_TE, PEER_TM), BF16), pltpu.VMEM((PEER_TE, PEER_TM), BF16),
                        pltpu.VMEM((d, PEER_TM), F32), pltpu.VMEM((PEER_TM, d), BF16),
                        pltpu.VMEM((PEER_HEADS, k1_blk, PEER_TM), F32),
                        pltpu.VMEM((PEER_HEADS, k1_blk, PEER_TM), F32),
                        pltpu.VMEM((PEER_HEADS, N_KEYS, PEER_TM), F32),
                        pltpu.VMEM((PEER_HEADS, N_KEYS, PEER_TM), F32)],
        compiler_params=_params("arbitrary"),
        name="peer_dense",
    )(h2, u_b, vt_b, tau, e1, s2, e2)


def _final_kernel(pt_ref, x1_ref, g2_ref, lg_ref, lb_ref, o_ref):
    p = pt_ref[...].T
    o_ref[...] = _layer_norm(DN_ALPHA * x1_ref[...] + g2_ref[...] * p, lg_ref[...], lb_ref[...])


def _peer_final(pt, x1, mod4, lg, lb, seq):
    t, d = x1.shape
    tm = 512
    tpb = seq // tm
    return pl.pallas_call(
        _final_kernel,
        grid=(t // tm,),
        in_specs=[pl.BlockSpec((d, tm), lambda i: (0, i)),
                  pl.BlockSpec((tm, d), lambda i: (i, 0)),
                  pl.BlockSpec((None, None, 1, d), lambda i: (i // tpb, 5, 0, 0)),
                  pl.BlockSpec((1, d), lambda i: (0, 0)),
                  pl.BlockSpec((1, d), lambda i: (0, 0))],
        out_specs=pl.BlockSpec((tm, d), lambda i: (i, 0)),
        out_shape=jax.ShapeDtypeStruct((t, d), F32),
        compiler_params=_params("arbitrary"),
        name="peer_final",
    )(pt, x1, mod4, lg, lb)


def _half_swap(w):
    half = w.shape[-1] // 2
    return jnp.concatenate([w[..., half:], w[..., :half]], axis=-1)


def _in_proj_layout(w, pad_to):
    o = np.cumsum([0, A_WIDTH, A_WIDTH, A_WIDTH, Q_LORA, KV_LORA, QK_ROPE, pad_to, pad_to])
    qa, ka, va, cq, ckv, kr, ga, gb = [w[..., o[n]:o[n + 1]] for n in range(8)]
    z64 = jnp.zeros(kr.shape[:-1] + (LANES - QK_ROPE,), w.dtype)
    ztail = jnp.zeros(kr.shape[:-1] + (IN_TN - 2 * LANES,), w.dtype)
    return jnp.concatenate([ga, gb, qa, ka, va, cq, ckv, kr, z64, _half_swap(kr), z64, ztail], axis=-1)


def _mla_weight_layout(w_uq, w_ukv):
    r = w_uq.shape[0]
    uq = w_uq.reshape(r, B_HEADS, QK_NOPE + QK_ROPE)
    nope, rope = uq[..., :QK_NOPE], uq[..., QK_NOPE:]
    z64 = jnp.zeros((r, B_HEADS, B_HEAD_PAD - QK_NOPE - QK_ROPE), w_uq.dtype)
    wq1 = jnp.concatenate([nope, rope, z64], axis=-1).reshape(r, B_HEADS * B_HEAD_PAD)
    wq2 = jnp.concatenate([_half_swap(rope), z64], axis=-1).reshape(r, B_HEADS * LANES)
    ukv = w_ukv.reshape(w_ukv.shape[0], B_HEADS, QK_NOPE + V_HEAD)
    wk = ukv[..., :QK_NOPE].reshape(-1, B_HEADS * QK_NOPE)
    wv = ukv[..., QK_NOPE:].reshape(-1, B_WIDTH)
    return wq1.astype(BF16), wq2.astype(BF16), wk.astype(BF16), wv.astype(BF16)


def kernel(x, c, positions, w_ada, b_ada, w_in, b_in, rel_bias, q_norm_g, kv_norm_g, w_uq, w_ukv,
           w_pa, w_pb, w_o, ln1_g, ln1_b, peer_wq, peer_keys, peer_u, peer_v, ln2_g, ln2_b):
    bsz, seq, d = x.shape
    t = bsz * seq
    assert d == GB0 - GA0 and seq % 1024 == 0 and w_ada.shape[0] == 1
    x2 = x.reshape(t, d)
    row = lambda a: a.reshape(1, -1).astype(F32)

    mod4 = _ada_mod(c, w_ada[0], b_ada[0]).reshape(bsz, 6, 1, d)
    cos_t, sin_t = _rope_tables(positions)

    w2 = _in_proj_layout(w_in[0], d).astype(BF16).reshape(d, Z_WIDTH // IN_TN, IN_TN).transpose(1, 0, 2)
    b2 = _in_proj_layout(b_in[0], d).reshape(1, Z_WIDTH).astype(F32)
    z = _in_proj(x2, mod4, w2, b2, row(q_norm_g[0]), row(kv_norm_g[0]), seq)

    wq1, wq2, wk, wv = _mla_weight_layout(w_uq[0], w_ukv[0])
    qb, kb, vb = _mla_up(z, cos_t, sin_t, wq1, wq2, wk, wv)
    yb = _mla_flash(qb, kb, vb, bsz, seq)
    ya = _band_attn(z, _band_bias(rel_bias[0]), bsz, seq)

    x1, h2 = _out_proj(ya, yb, z, x2, mod4, w_pa[0].astype(BF16), w_pb[0].astype(BF16),
                       w_o[0].astype(BF16), row(ln1_g[0]), row(ln1_b[0]), seq)

    wqt = peer_wq[0].astype(BF16).T
    keys_b = peer_keys[0].reshape(2 * PEER_HEADS, N_KEYS, PEER_HALF).astype(BF16)
    tau, e1, s2, e2 = _peer_topk(h2, wqt, keys_b)
    vt_b = peer_v[0].astype(BF16).reshape(-1, PEER_TE, d).transpose(0, 2, 1)
    pt = _peer_dense(h2, peer_u[0].astype(BF16), vt_b, tau, e1, s2, e2)
    out = _peer_final(pt, x1, mod4, row(ln2_g[0]), row(ln2_b[0]), seq)
    return out.reshape(bsz, seq, d)
```
